```python
import jax, jax.numpy as jnp
from jax import lax
import numpy as np

D_MODEL = 2048
BATCH = 32
SEQ = 256
DEPTH = 4
DEC_BATCH = 4
DEC_SEQ = 4096
PAST_LEN = 512

GRID_W = 64
H_RET = 8
D_RET = 128
RET_W = H_RET * D_RET
RET_CHUNK = 128
H_MLA = 8
D_NOPE = 128
D_ROPE = 64
D_V = 128
Q_RANK = 512
KV_RANK = 256
Q_BLOCK = 128
IN_COLS = 4 * RET_W + Q_RANK + KV_RANK + D_ROPE
N_EXPERTS = 16
N_GROUPS = 4
EXPERTS_PER_GROUP = N_EXPERTS // N_GROUPS
TOP_K = 2
GROUP_SCORE_TOPK = 2
D_EXPERT = 1024
MOE_BLOCK = 128

ROPE_BASE = 10000.0
EPS = 1e-6
F32 = jnp.float32

kernel_name = "hybrid_retention_mla_moe_diffusion_step"


def rmsnorm(x, g):
    xf = x.astype(F32)
    y = xf * lax.rsqrt(jnp.mean(xf * xf, axis=-1, keepdims=True) + EPS)
    return (y * g.astype(F32)).astype(x.dtype)


def head_norm(o):
    mu = jnp.mean(o, axis=-1, keepdims=True)
    var = jnp.mean(jnp.square(o - mu), axis=-1, keepdims=True)
    return (o - mu) * lax.rsqrt(var + EPS)


def axial_rope_tables(n, dim):
    t = jnp.arange(n)
    rows = (t // GRID_W).astype(F32)
    cols = (t % GRID_W).astype(F32)
    nf = dim // 4
    inv = jnp.power(ROPE_BASE, -jnp.arange(nf, dtype=F32) / nf)
    ang = jnp.concatenate([rows[:, None] * inv, cols[:, None] * inv], axis=-1)
    return jnp.cos(ang), jnp.sin(ang)


def apply_rope(x, cos, sin):
    half = x.shape[-1] // 2
    x1 = x[..., :half].astype(F32)
    x2 = x[..., half:].astype(F32)
    return jnp.concatenate([x1 * cos - x2 * sin, x2 * cos + x1 * sin], axis=-1).astype(x.dtype)


def retention_scan(q, k, v, log_gamma, s0):
    B, H, n, d = q.shape
    nc = n // RET_CHUNK
    idx = jnp.arange(RET_CHUNK, dtype=F32)
    diff = idx[:, None] - idx[None, :]
    lg = log_gamma[:, None, None]
    decay = jnp.where(diff >= 0, jnp.exp(jnp.maximum(diff, 0.0) * lg), 0.0)
    xi = jnp.exp((idx + 1.0) * log_gamma[:, None])
    zeta = jnp.exp((RET_CHUNK - 1.0 - idx) * log_gamma[:, None])
    g_chunk = jnp.exp(RET_CHUNK * log_gamma)

    def to_chunks(a):
        return a.reshape(B, H, nc, RET_CHUNK, d).transpose(2, 0, 1, 3, 4)

    def step(S, qkv):
        qc, kc, vc = qkv
        inner = jnp.einsum('bhid,bhjd->bhij', qc, kc) * decay
        o = (jnp.einsum('bhij,bhje->bhie', inner, vc)
             + jnp.einsum('bhid,bhde->bhie', qc * xi[:, :, None], S))
        S = g_chunk[:, None, None] * S + jnp.einsum('bhjd,bhje->bhde', kc * zeta[:, :, None], vc)
        return S, o

    s_fin, o = lax.scan(step, s0.astype(F32), (to_chunks(q), to_chunks(k), to_chunks(v)))
    o = o.transpose(1, 2, 0, 3, 4).reshape(B, H, n, d)
    return o, s_fin


def bidir_retention(q, k, v, log_gamma2, s0_f, s0_b):
    o_f, s_f = retention_scan(q, k, v, log_gamma2[0], s0_f)
    flip = lambda a: jnp.flip(a, axis=2)
    o_b, s_b = retention_scan(flip(q), flip(k), flip(v), log_gamma2[1], s0_b)
    return o_f + flip(o_b), s_f, s_b


def mla_attention(q_nope, q_rope, k_nope, k_rope, v):
    B, nq, H, _ = q_nope.shape
    nb = nq // Q_BLOCK
    scale = (D_NOPE + D_ROPE) ** -0.5

    def blocks(a):
        return a.reshape(B, nb, Q_BLOCK, H, a.shape[-1]).transpose(1, 0, 2, 3, 4)

    def attend(args):
        qn, qr = args
        s = (jnp.einsum('bqhd,bkhd->bhqk', qn, k_nope)
             + jnp.einsum('bqhd,bkd->bhqk', qr, k_rope)).astype(F32) * scale
        p = jax.nn.softmax(s, axis=-1).astype(v.dtype)
        return jnp.einsum('bhqk,bkhd->bqhd', p, v)

    o = lax.map(attend, (blocks(q_nope), blocks(q_rope)))
    return o.transpose(1, 0, 2, 3, 4).reshape(B, nq, H * D_V)


def token_mixers(h, w_in, log_gamma2, q_norm_g, kv_norm_g, w_uq, w_ukv, w_out, ctx):
    B, n, _ = h.shape
    proj = h @ w_in
    cuts = [RET_W, 2 * RET_W, 3 * RET_W, 4 * RET_W, 4 * RET_W + Q_RANK, 4 * RET_W + Q_RANK + KV_RANK]
    q_r, k_r, v_r, g_r, c_q, c_kv, k_rope = jnp.split(proj, cuts, axis=-1)
    q_r = q_r.reshape(B, n, H_RET, D_RET)
    k_r = k_r.reshape(B, n, H_RET, D_RET) * (D_RET ** -0.5)
    v_r = v_r.reshape(B, n, H_RET, D_RET)
    ckv = rmsnorm(c_kv, kv_norm_g)
    qm = (rmsnorm(c_q, q_norm_g) @ w_uq).reshape(B, n, H_MLA, D_NOPE + D_ROPE)
    q_nope, q_rope = qm[..., :D_NOPE], qm[..., D_NOPE:]

    if ctx is None:
        s0_f = jnp.zeros((B, H_RET, D_RET, D_RET), F32)
        s0_b = s0_f
        keys_ckv, keys_krope = ckv, k_rope
    else:
        ckv_ctx, krope_ctx, s0_f, s0_b = ctx
        cos_r, sin_r = axial_rope_tables(n, D_RET)
        q_r = apply_rope(q_r, cos_r[:, None, :], sin_r[:, None, :])
        k_r = apply_rope(k_r, cos_r[:, None, :], sin_r[:, None, :])
        cos_m, sin_m = axial_rope_tables(n, D_ROPE)
        q_rope = apply_rope(q_rope, cos_m[:, None, :], sin_m[:, None, :])
        k_rope_lat = apply_rope(k_rope, cos_m, sin_m)
        keys_ckv = jnp.concatenate([ckv_ctx.astype(ckv.dtype), ckv], axis=1)
        keys_krope = jnp.concatenate([krope_ctx.astype(k_rope.dtype), k_rope_lat], axis=1)

    to_bh = lambda a: a.transpose(0, 2, 1, 3).astype(F32)
    o_ret, s_f, s_b = bidir_retention(to_bh(q_r), to_bh(k_r), to_bh(v_r), log_gamma2, s0_f, s0_b)
    o_ret = head_norm(o_ret).transpose(0, 2, 1, 3).reshape(B, n, RET_W).astype(h.dtype) * jax.nn.silu(g_r)

    n_k = keys_ckv.shape[1]
    kv = (keys_ckv @ w_ukv).reshape(B, n_k, H_MLA, D_NOPE + D_V)
    k_nope, v = kv[..., :D_NOPE], kv[..., D_NOPE:]
    o_mla = mla_attention(q_nope, q_rope, k_nope, keys_krope, v)

    out = jnp.concatenate([o_ret, o_mla], axis=-1) @ w_out
    return out, ckv, k_rope, s_f, s_b


def group_limited_route(h2d, w_router, b_router):
    T = h2d.shape[0]
    s = jax.nn.sigmoid(h2d.astype(F32) @ w_router.astype(F32))
    sel = (s + b_router.astype(F32)).reshape(T, N_GROUPS, EXPERTS_PER_GROUP)
    group_score = lax.top_k(sel, GROUP_SCORE_TOPK)[0].sum(-1)
    g = jnp.argmax(group_score, axis=-1)
    within = sel[jnp.arange(T), g]
    _, local = lax.top_k(within, TOP_K)
    expert = (g[:, None] * EXPERTS_PER_GROUP + local).astype(jnp.int32)
    w = jnp.take_along_axis(s, expert, axis=1)
    return expert, w / jnp.sum(w, axis=-1, keepdims=True)


def moe_ffn(h2d, w_router, b_router, w_gate, w_up, w_down):
    T, D = h2d.shape
    expert, gates = group_limited_route(h2d, w_router, b_router)
    tk = T * TOP_K
    flat_e = expert.reshape(tk)
    flat_tok = jnp.repeat(jnp.arange(T, dtype=jnp.int32), TOP_K)
    flat_g = gates.reshape(tk)
    order = jnp.argsort(flat_e)
    e_sorted = flat_e[order]
    counts = jnp.bincount(flat_e, length=N_EXPERTS)
    padded = (counts + MOE_BLOCK - 1) // MOE_BLOCK * MOE_BLOCK
    start = jnp.cumsum(counts) - counts
    pend = jnp.cumsum(padded)
    pstart = pend - padded
    dest = pstart[e_sorted] + (jnp.arange(tk, dtype=jnp.int32) - start[e_sorted])
    n_blocks = -(-tk // MOE_BLOCK) + N_EXPERTS
    n_pad = n_blocks * MOE_BLOCK
    buf_tok = jnp.full((n_pad,), T, jnp.int32).at[dest].set(flat_tok[order])
    buf_gate = jnp.zeros((n_pad,), F32).at[dest].set(flat_g[order])
    block_start = jnp.arange(n_blocks, dtype=jnp.int32) * MOE_BLOCK
    block_e = jnp.minimum(jnp.searchsorted(pend, block_start, side='right'), N_EXPERTS - 1)
    x_pad = jnp.concatenate([h2d, jnp.zeros((1, D), h2d.dtype)], axis=0)

    def expert_block(args):
        tok, e = args
        xb = x_pad[tok]
        return (jax.nn.silu(xb @ w_gate[e]) * (xb @ w_up[e])) @ w_down[e]

    out = lax.map(expert_block, (buf_tok.reshape(n_blocks, MOE_BLOCK), block_e))
    out = out.reshape(n_pad, D).astype(F32) * buf_gate[:, None]
    y = jax.ops.segment_sum(out, buf_tok, num_segments=T + 1)[:T]
    return y.astype(h2d.dtype)


def trunk_layer(x, cond, lw, w_router, b_router, ctx):
    (w_ada, b_ada, n_attn, n_ffn, w_in, log_gamma2, q_g, kv_g, w_uq, w_ukv, w_out,
     w_eg, w_eu, w_ed) = lw
    mod = jax.nn.silu(cond) @ w_ada + b_ada
    sh1, sc1, g1, sh2, sc2, g2 = [m[:, None, :] for m in jnp.split(mod, 6, axis=-1)]
    h = rmsnorm(x, n_attn) * (1.0 + sc1) + sh1
    mix, ckv, krope, s_f, s_b = token_mixers(h, w_in, log_gamma2, q_g, kv_g, w_uq, w_ukv, w_out, ctx)
    x = x + g1 * mix
    h2 = rmsnorm(x, n_ffn) * (1.0 + sc2) + sh2
    B, n, D = h2.shape
    ffn = moe_ffn(h2.reshape(B * n, D), w_router, b_router, w_eg, w_eu, w_ed).reshape(B, n, D)
    x = x + g2 * ffn
    return x, ckv, krope, s_f, s_b


def setup_inputs(seed: int = 0) -> dict:
    key = jax.random.key(seed)
    ks = jax.random.split(key, 24)

    def nrm(k, shape, scale):
        return jax.random.normal(k, shape, F32) * scale

    base_logit = jnp.log(jnp.power(2.0, 5.0 + jnp.arange(H_RET, dtype=F32)) - 1.0)
    return {
        "x_prompt": nrm(ks[0], (BATCH, SEQ, D_MODEL), 1.0),
        "x_sample": nrm(ks[1], (DEC_BATCH, DEC_SEQ, D_MODEL), 1.0),
        "c": nrm(ks[2], (DEC_BATCH, D_MODEL), 1.0),
        "cache_ckv": nrm(ks[3], (DEC_BATCH, DEPTH, PAST_LEN, KV_RANK), 1.0),
        "cache_krope": nrm(ks[4], (DEC_BATCH, DEPTH, PAST_LEN, D_ROPE), 1.0),
        "state_ret": nrm(ks[5], (DEC_BATCH, DEPTH, 2, H_RET, D_RET, D_RET), 0.1),
        "c_ctx": nrm(ks[6], (D_MODEL,), 1.0),
        "w_ada": nrm(ks[7], (DEPTH, D_MODEL, 6 * D_MODEL), 0.5 * D_MODEL ** -0.5),
        "b_ada": nrm(ks[8], (DEPTH, 6 * D_MODEL), 0.02),
        "norm_attn": 1.0 + nrm(ks[9], (DEPTH, D_MODEL), 0.02),
        "norm_ffn": 1.0 + nrm(ks[10], (DEPTH, D_MODEL), 0.02),
        "w_in": nrm(ks[11], (DEPTH, D_MODEL, IN_COLS), D_MODEL ** -0.5),
        "ret_decay_logit": base_logit[None, None, :] + nrm(ks[12], (DEPTH, 2, H_RET), 0.1),
        "q_norm": 1.0 + nrm(ks[13], (DEPTH, Q_RANK), 0.02),
        "kv_norm": 1.0 + nrm(ks[14], (DEPTH, KV_RANK), 0.02),
        "w_uq": nrm(ks[15], (DEPTH, Q_RANK, H_MLA * (D_NOPE + D_ROPE)), Q_RANK ** -0.5),
        "w_ukv": nrm(ks[16], (DEPTH, KV_RANK, H_MLA * (D_NOPE + D_V)), KV_RANK ** -0.5),
        "w_out": nrm(ks[17], (DEPTH, D_MODEL, D_MODEL), D_MODEL ** -0.5),
        "w_router": nrm(ks[18], (D_MODEL, N_EXPERTS), D_MODEL ** -0.5),
        "b_router": nrm(ks[19], (N_EXPERTS,), 0.01),
        "w_exp_gate": nrm(ks[20], (DEPTH, N_EXPERTS, D_MODEL, D_EXPERT), D_MODEL ** -0.5),
        "w_exp_up": nrm(ks[21], (DEPTH, N_EXPERTS, D_MODEL, D_EXPERT), D_MODEL ** -0.5),
        "w_exp_down": nrm(ks[22], (DEPTH, N_EXPERTS, D_EXPERT, D_MODEL), D_EXPERT ** -0.5),
        "final_norm": 1.0 + nrm(ks[23], (D_MODEL,), 0.02),
    }


def reference(x_prompt, x_sample, c, cache_ckv, cache_krope, state_ret, c_ctx, w_ada, b_ada,
              norm_attn, norm_ffn, w_in, ret_decay_logit, q_norm, kv_norm, w_uq, w_ukv, w_out,
              w_router, b_router, w_exp_gate, w_exp_up, w_exp_down, final_norm):
    xp = x_prompt
    xs = x_sample
    ckv_layers, krope_layers, state_layers = [], [], []
    for l in range(DEPTH):
        log_gamma2 = jax.nn.log_sigmoid(ret_decay_logit[l].astype(F32))
        lw = (w_ada[l], b_ada[l], norm_attn[l], norm_ffn[l], w_in[l], log_gamma2, q_norm[l], kv_norm[l],
              w_uq[l], w_ukv[l], w_out[l], w_exp_gate[l], w_exp_up[l], w_exp_down[l])
        xp, ckv, krope, s_f, s_b = trunk_layer(xp, c_ctx[None, :], lw, w_router, b_router, None)
        ckv_layers.append(ckv)
        krope_layers.append(krope)
        state_layers.append(jnp.stack([s_f, s_b], axis=1).astype(x_prompt.dtype))
        ctx = (cache_ckv[:, l], cache_krope[:, l], state_ret[:, l, 0], state_ret[:, l, 1])
        xs, _, _, _, _ = trunk_layer(xs, c, lw, w_router, b_router, ctx)
    y_prompt = rmsnorm(xp, final_norm)
    y_sample = rmsnorm(xs, final_norm)
    new_cache_ckv = jnp.stack(ckv_layers, axis=1)
    new_cache_krope = jnp.stack(krope_layers, axis=1)
    new_state_ret = jnp.stack(state_layers, axis=1)
    return (y_prompt, y_sample, new_cache_ckv, new_cache_krope, new_state_ret)
```

```python
import functools

import jax
import jax.numpy as jnp
from jax import lax
from jax.experimental import pallas as pl
from jax.experimental.pallas import tpu as pltpu

F32 = jnp.float32
BF16 = jnp.bfloat16
I32 = jnp.int32

D_MODEL = 2048
GRID_W = 64
H_RET = 8
D_RET = 128
RET_W = H_RET * D_RET
RET_CHUNK = 128
H_MLA = 8
D_NOPE = 128
D_ROPE = 64
D_V = 128
Q_RANK = 512
KV_RANK = 256
N_EXPERTS = 16
N_GROUPS = 4
EXPERTS_PER_GROUP = N_EXPERTS // N_GROUPS
D_EXPERT = 1024
ROPE_BASE = 10000.0
EPS = 1e-6

LANES = 128
Q_HEAD_W = 2 * LANES
LAT_W = 1024
IN_W = 4 * RET_W + LAT_W
ROW_TILE = 512
OUT_TILE = 256
Q_TILE = 256
MOE_TILE = 256
GATHER_TILE = 256
VMEM_LIMIT = 56 * 1024 * 1024


def _params(sem, **kw):
    return pltpu.CompilerParams(dimension_semantics=sem, vmem_limit_bytes=VMEM_LIMIT, **kw)


def _silu(x):
    return x * jax.nn.sigmoid(x)


def _mod_row(i, tile, n_ctx, seq_lat):
    r0 = i * tile
    return jnp.where(r0 < n_ctx, 0, 1 + (r0 - n_ctx) // seq_lat)


def _pos_block(i, tile, n_ctx, seq_lat):
    r0 = i * tile
    return jnp.where(r0 < n_ctx, 0, 1 + ((r0 - n_ctx) % seq_lat) // tile)


def _ada_kernel(c_ref, w_ref, b_ref, o_ref):
    a = _silu(c_ref[...]).astype(BF16)
    o_ref[...] = jnp.dot(a, w_ref[...].astype(BF16), preferred_element_type=F32) + b_ref[...]


def _ada_call(cond, w_ada, b_ada):
    depth, d, n6 = w_ada.shape
    tn = 1536
    return pl.pallas_call(
        _ada_kernel,
        grid=(depth, n6 // tn),
        in_specs=[pl.BlockSpec((8, d), lambda l, j: (0, 0)),
                  pl.BlockSpec((None, d, tn), lambda l, j: (l, 0, j)),
                  pl.BlockSpec((None, 1, tn), lambda l, j: (l, 0, j))],
        out_specs=pl.BlockSpec((None, 8, tn), lambda l, j: (l, 0, j)),
        out_shape=jax.ShapeDtypeStruct((depth, 8, n6), F32),
        compiler_params=_params(("arbitrary", "arbitrary")),
        name="adaln",
    )(cond, w_ada, b_ada.reshape(depth, 1, n6))


def _inproj_kernel(x_ref, sh_ref, sc_ref, ng_ref, w_ref, qg_ref, kvg_ref,
                   qkvg_ref, cq_ref, ckv_ref, kr_ref, h_scr, *, nj):
    j = pl.program_id(1)

    @pl.when(j == 0)
    def _():
        x = x_ref[...]
        y = x * lax.rsqrt(jnp.mean(x * x, axis=-1, keepdims=True) + EPS) * ng_ref[...]
        h_scr[...] = (y * (1.0 + sc_ref[...]) + sh_ref[...]).astype(BF16)

    acc = jnp.dot(h_scr[...], w_ref[...], preferred_element_type=F32)

    @pl.when(j < nj - 1)
    def _():
        qkvg_ref[...] = acc.astype(BF16)

    @pl.when(j == nj - 1)
    def _():
        cq = acc[:, :Q_RANK]
        cq = cq * lax.rsqrt(jnp.mean(cq * cq, axis=-1, keepdims=True) + EPS) * qg_ref[...]
        cq_ref[...] = cq.astype(BF16)
        ckv = acc[:, Q_RANK:Q_RANK + KV_RANK]
        ckv_ref[...] = ckv * lax.rsqrt(jnp.mean(ckv * ckv, axis=-1, keepdims=True) + EPS) * kvg_ref[...]
        kr_ref[...] = acc[:, Q_RANK + KV_RANK:Q_RANK + KV_RANK + LANES]


def _inproj_call(x, mod3, ng, w_in_p, qg, kvg, n_ctx, seq_lat):
    n = x.shape[0]
    tm = ROW_TILE
    nj = IN_W // LAT_W
    mrow = functools.partial(_mod_row, tile=tm, n_ctx=n_ctx, seq_lat=seq_lat)
    return pl.pallas_call(
        functools.partial(_inproj_kernel, nj=nj),
        grid=(n // tm, nj),
        in_specs=[pl.BlockSpec((tm, D_MODEL), lambda i, j: (i, 0)),
                  pl.BlockSpec((None, 1, D_MODEL), lambda i, j: (mrow(i), 0, 0)),
                  pl.BlockSpec((None, 1, D_MODEL), lambda i, j: (mrow(i), 0, 1)),
                  pl.BlockSpec((1, D_MODEL), lambda i, j: (0, 0)),
                  pl.BlockSpec((D_MODEL, LAT_W), lambda i, j: (0, j)),
                  pl.BlockSpec((1, Q_RANK), lambda i, j: (0, 0)),
                  pl.BlockSpec((1, KV_RANK), lambda i, j: (0, 0))],
        out_specs=[pl.BlockSpec((tm, LAT_W), lambda i, j: (i, jnp.minimum(j, nj - 2))),
                   pl.BlockSpec((tm, Q_RANK), lambda i, j: (i, 0)),
                   pl.BlockSpec((tm, KV_RANK), lambda i, j: (i, 0)),
                   pl.BlockSpec((tm, LANES), lambda i, j: (i, 0))],
        out_shape=[jax.ShapeDtypeStruct((n, 4 * RET_W), BF16),
                   jax.ShapeDtypeStruct((n, Q_RANK), BF16),
                   jax.ShapeDtypeStruct((n, KV_RANK), F32),
                   jax.ShapeDtypeStruct((n, LANES), F32)],
        scratch_shapes=[pltpu.VMEM((tm, D_MODEL), BF16)],
        compiler_params=_params(("arbitrary", "arbitrary")),
        name="inproj",
    )(x, mod3, mod3, ng, w_in_p, qg, kvg)


def _retention_kernel(lg_ref, *refs, n, hb, rope, has_s0, aliased, emit_state):
    it = iter(refs)
    q_ref, k_ref, v_ref, g_ref = next(it), next(it), next(it), next(it)
    s0_ref = next(it) if has_s0 else None
    cos_ref, sin_ref = (next(it), next(it)) if rope else (None, None)
    if aliased:
        next(it)
    o_ref = next(it)
    sfin_ref = next(it) if emit_state else None
    sf_scr, kr_scr = next(it), next(it)

    C = RET_CHUNK
    nc = n // C
    hblk = pl.program_id(1)
    diff = (lax.broadcasted_iota(I32, (C, C), 0) - lax.broadcasted_iota(I32, (C, C), 1)).astype(F32)
    col = lax.broadcasted_iota(I32, (C, 1), 0).astype(F32)
    k_scale = D_RET ** -0.5

    for hh in range(hb):
        head = hblk * hb + hh
        lgf = lg_ref[0, head]
        lgb = lg_ref[1, head]
        dmat = (jnp.where(diff >= 0, jnp.exp(jnp.maximum(diff, 0.0) * lgf), 0.0)
                + jnp.where(diff <= 0, jnp.exp(jnp.maximum(-diff, 0.0) * lgb), 0.0))
        xi_f = jnp.exp((col + 1.0) * lgf)
        zeta_f = jnp.exp((C - 1.0 - col) * lgf)
        xi_b = jnp.exp((C - col) * lgb)
        zeta_b = jnp.exp(col * lgb)
        g_f = jnp.exp(jnp.full((1, C), float(C), F32) * lgf)
        g_b = jnp.exp(jnp.full((1, C), float(C), F32) * lgb)
        ls = slice(hh * C, (hh + 1) * C)

        def rows(c):
            return pl.ds(pl.multiple_of(c * C, C), C)

        def rot(x, c):
            if not rope:
                return x
            return x * cos_ref[rows(c), :] + pltpu.roll(x, C // 2, 1) * sin_ref[rows(c), :]

        def kv_outer(kc, zeta, vc):
            return jnp.dot((kc * zeta).T.astype(BF16), vc, preferred_element_type=F32)

        def fwd(c, s):
            kc = rot(k_ref[rows(c), ls].astype(F32) * k_scale, c)
            kr_scr[rows(c), :] = kc
            sf_scr[c] = s
            return g_f * s + kv_outer(kc, zeta_f, v_ref[rows(c), ls])

        def bwd(i, s):
            c = nc - 1 - i
            qc = rot(q_ref[rows(c), ls].astype(F32), c)
            kc = kr_scr[rows(c), :]
            vc = v_ref[rows(c), ls]
            inner = lax.dot_general(qc.astype(BF16), kc.astype(BF16), (((1,), (1,)), ((), ())),
                                    preferred_element_type=F32) * dmat
            o = (jnp.dot(inner.astype(BF16), vc, preferred_element_type=F32)
                 + jnp.dot((qc * xi_f).astype(BF16), sf_scr[c].astype(BF16), preferred_element_type=F32)
                 + jnp.dot((qc * xi_b).astype(BF16), s.astype(BF16), preferred_element_type=F32))
            dev = o - jnp.mean(o, axis=-1, keepdims=True)
            on = dev * lax.rsqrt(jnp.mean(dev * dev, axis=-1, keepdims=True) + EPS)
            gate = g_ref[rows(c), ls].astype(F32)
            o_ref[rows(c), ls] = (on * _silu(gate)).astype(BF16)
            return g_b * s + kv_outer(kc, zeta_b, vc)

        if has_s0:
            s_f0, s_b0 = s0_ref[0, hh], s0_ref[1, hh]
        else:
            s_f0 = s_b0 = jnp.zeros((C, C), F32)
        s_f = lax.fori_loop(0, nc, fwd, s_f0)
        s_b = lax.fori_loop(0, nc, bwd, s_b0)
        if emit_state:
            sfin_ref[0, hh] = s_f
            sfin_ref[1, hh] = s_b


def _retention_call(lg2, qkvg, batch, n, row_blk0, hb, *, s0=None, tables=None, prev=None, emit_state=False):
    n_tok = qkvg.shape[0]
    hblks = H_RET // hb
    w = hb * D_RET

    def qspec(part):
        return pl.BlockSpec((n, w), lambda b, h, lg: (row_blk0 + b, part * hblks + h))

    in_specs = [qspec(0), qspec(1), qspec(2), qspec(3)]
    args = [qkvg, qkvg, qkvg, qkvg]
    if s0 is not None:
        in_specs.append(pl.BlockSpec((None, 2, hb, D_RET, D_RET), lambda b, h, lg: (b, 0, h, 0, 0)))
        args.append(s0)
    if tables is not None:
        in_specs += [pl.BlockSpec((n, D_RET), lambda b, h, lg: (0, 0))] * 2
        args += list(tables)
    aliases = {}
    if prev is not None:
        in_specs.append(pl.BlockSpec(memory_space=pl.ANY))
        aliases = {len(args) + 1: 0}
        args.append(prev)
    out_specs = [pl.BlockSpec((n, w), lambda b, h, lg: (row_blk0 + b, h))]
    out_shape = [jax.ShapeDtypeStruct((n_tok, RET_W), BF16)]
    if emit_state:
        out_specs.append(pl.BlockSpec((None, 2, hb, D_RET, D_RET), lambda b, h, lg: (b, 0, h, 0, 0)))
        out_shape.append(jax.ShapeDtypeStruct((batch, 2, H_RET, D_RET, D_RET), F32))
    kern = functools.partial(_retention_kernel, n=n, hb=hb, rope=tables is not None, has_s0=s0 is not None,
                             aliased=prev is not None, emit_state=emit_state)
    return pl.pallas_call(
        kern,
        grid_spec=pltpu.PrefetchScalarGridSpec(
            num_scalar_prefetch=1, grid=(batch, hblks), in_specs=in_specs, out_specs=out_specs,
            scratch_shapes=[pltpu.VMEM((n // RET_CHUNK, D_RET, D_RET), F32), pltpu.VMEM((n, D_RET), F32)]),
        out_shape=out_shape,
        input_output_aliases=aliases,
        compiler_params=_params(("arbitrary", "arbitrary")),
        name="retention",
    )(lg2, *args)


def _qproj_kernel(cq_ref, w_ref, cos_ref, sin_ref, q_ref):
    scale = (D_NOPE + D_ROPE) ** -0.5
    acc = jnp.dot(cq_ref[...], w_ref[...], preferred_element_type=F32)
    for h in range(H_MLA):
        base = h * Q_HEAD_W
        q_ref[:, base:base + LANES] = (acc[:, base:base + LANES] * scale).astype(BF16)
        tail = acc[:, base + LANES:base + Q_HEAD_W]
        rot = tail * cos_ref[...] + pltpu.roll(tail, D_ROPE, 1) * sin_ref[...]
        q_ref[:, base + LANES:base + Q_HEAD_W] = (rot * scale).astype(BF16)


def _qproj_call(cqn, w_uq_p, cos_q, sin_q, n_ctx, seq_lat):
    n = cqn.shape[0]
    tm = ROW_TILE
    pblk = functools.partial(_pos_block, tile=tm, n_ctx=n_ctx, seq_lat=seq_lat)
    return pl.pallas_call(
        _qproj_kernel,
        grid=(n // tm,),
        in_specs=[pl.BlockSpec((tm, Q_RANK), lambda i: (i, 0)),
                  pl.BlockSpec((Q_RANK, H_MLA * Q_HEAD_W), lambda i: (0, 0)),
                  pl.BlockSpec((tm, LANES), lambda i: (pblk(i), 0)),
                  pl.BlockSpec((tm, LANES), lambda i: (pblk(i), 0))],
        out_specs=pl.BlockSpec((tm, H_MLA * Q_HEAD_W), lambda i: (i, 0)),
        out_shape=jax.ShapeDtypeStruct((n, H_MLA * Q_HEAD_W), BF16),
        compiler_params=_params(("arbitrary",)),
        name="qproj",
    )(cqn, w_uq_p, cos_q, sin_q)


def _kvproj_kernel(ckv_ref, kr_ref, cos_ref, sin_ref, wkt_ref, wv_ref, eye_ref, kt_ref, v_ref, *, hb):
    nt = (((1,), (1,)), ((), ()))
    keys = ckv_ref[...]
    kr = kr_ref[...]
    rot = (kr * cos_ref[...] + pltpu.roll(kr, D_ROPE, 1) * sin_ref[...]).astype(BF16)
    rope_t = lax.dot_general(eye_ref[...], rot, nt, preferred_element_type=F32).astype(BF16)
    for hh in range(hb):
        kt_ref[hh, :D_NOPE, :] = lax.dot_general(wkt_ref[hh], keys, nt, preferred_element_type=F32).astype(BF16)
        kt_ref[hh, D_NOPE:, :] = rope_t
        v_ref[hh] = jnp.dot(keys, wv_ref[hh], preferred_element_type=F32).astype(BF16)


def _kvproj_call(keys_ckv, keys_kr, cos_k, sin_k, wkt, wv, eye, hb):
    batch, nk, _ = keys_ckv.shape
    hblks = H_MLA // hb
    return pl.pallas_call(
        functools.partial(_kvproj_kernel, hb=hb),
        grid=(batch, hblks),
        in_specs=[pl.BlockSpec((None, nk, KV_RANK), lambda b, h: (b, 0, 0)),
                  pl.BlockSpec((None, nk, LANES), lambda b, h: (b, 0, 0)),
                  pl.BlockSpec((nk, LANES), lambda b, h: (0, 0)),
                  pl.BlockSpec((nk, LANES), lambda b, h: (0, 0)),
                  pl.BlockSpec((hb, D_NOPE, KV_RANK), lambda b, h: (h, 0, 0)),
                  pl.BlockSpec((hb, KV_RANK, D_V), lambda b, h: (h, 0, 0)),
                  pl.BlockSpec((LANES, LANES), lambda b, h: (0, 0))],
        out_specs=[pl.BlockSpec((None, hb, Q_HEAD_W, nk), lambda b, h: (b, h, 0, 0)),
                   pl.BlockSpec((None, hb, nk, D_V), lambda b, h: (b, h, 0, 0))],
        out_shape=[jax.ShapeDtypeStruct((batch, H_MLA, Q_HEAD_W, nk), BF16),
                   jax.ShapeDtypeStruct((batch, H_MLA, nk, D_V), BF16)],
        compiler_params=_params(("arbitrary", "arbitrary")),
        name="kvproj",
    )(keys_ckv, keys_kr, cos_k, sin_k, wkt, wv, eye)


def _attn_kernel(*refs, hb, aliased):
    q_ref, kt_ref, v_ref = refs[:3]
    o_ref = refs[-1]
    del aliased
    for hh in range(hb):
        q = q_ref[:, hh * Q_HEAD_W:(hh + 1) * Q_HEAD_W]
        s = jnp.dot(q, kt_ref[hh], preferred_element_type=F32)
        p = jnp.exp(s - jnp.max(s, axis=-1, keepdims=True))
        den = jnp.sum(p, axis=-1, keepdims=True)
        o = jnp.dot(p.astype(BF16), v_ref[hh], preferred_element_type=F32)
        o_ref[:, hh * D_V:(hh + 1) * D_V] = (o / den).astype(BF16)


def _attn_call(q, kt, v, n, row0, tq, hb, prev=None):
    n_tok = q.shape[0]
    batch, _, _, nk = kt.shape
    hblks = H_MLA // hb
    nq = n // tq
    blk0 = row0 // tq

    def rmap(b, h, i):
        return (blk0 + b * nq + i, h)

    in_specs = [pl.BlockSpec((tq, hb * Q_HEAD_W), rmap),
                pl.BlockSpec((None, hb, Q_HEAD_W, nk), lambda b, h, i: (b, h, 0, 0)),
                pl.BlockSpec((None, hb, nk, D_V), lambda b, h, i: (b, h, 0, 0))]
    args = [q, kt, v]
    aliases = {}
    if prev is not None:
        in_specs.append(pl.BlockSpec(memory_space=pl.ANY))
        aliases = {len(args): 0}
        args.append(prev)
    return pl.pallas_call(
        functools.partial(_attn_kernel, hb=hb, aliased=prev is not None),
        grid=(batch, hblks, nq),
        in_specs=in_specs,
        out_specs=pl.BlockSpec((tq, hb * D_V), rmap),
        out_shape=jax.ShapeDtypeStruct((n_tok, H_MLA * D_V), BF16),
        input_output_aliases=aliases,
        compiler_params=_params(("arbitrary", "arbitrary", "arbitrary")),
        name="mla_attention",
    )(*args)


def _route_rows(s_rows, sel_rows):
    epg = EXPERTS_PER_GROUP
    best_g = best_v = None
    for g in range(N_GROUPS):
        v = sel_rows[g * epg:(g + 1) * epg]
        top2 = None
        for a in range(epg):
            for b in range(a + 1, epg):
                pair = v[a] + v[b]
                top2 = pair if top2 is None else jnp.maximum(top2, pair)
        if g == 0:
            best_v, best_g = top2, jnp.zeros(top2.shape, I32)
        else:
            upd = top2 > best_v
            best_v = jnp.where(upd, top2, best_v)
            best_g = jnp.where(upd, g, best_g)

    def pick(rows_, i):
        out = rows_[i]
        for g in range(1, N_GROUPS):
            out = jnp.where(best_g == g, rows_[g * epg + i], out)
        return out

    w = [pick(sel_rows, i) for i in range(epg)]
    sv = [pick(s_rows, i) for i in range(epg)]
    m0, i0, s0 = w[0], jnp.zeros(w[0].shape, I32), sv[0]
    for i in range(1, epg):
        upd = w[i] > m0
        m0, i0, s0 = jnp.where(upd, w[i], m0), jnp.where(upd, i, i0), jnp.where(upd, sv[i], s0)
    m1 = jnp.full(w[0].shape, -jnp.inf, F32)
    i1, s1 = jnp.zeros(w[0].shape, I32), sv[0]
    for i in range(epg):
        upd = (i0 != i) & (w[i] > m1)
        m1, i1, s1 = jnp.where(upd, w[i], m1), jnp.where(upd, i, i1), jnp.where(upd, sv[i], s1)
    tot = s0 + s1
    return best_g * epg + i0, best_g * epg + i1, s0 / tot, s1 / tot


def _outproj_kernel(x_ref, mr_ref, mm_ref, wo_ref, g1_ref, sc_ref, sh_ref, ng_ref, wr_ref, br_ref,
                    x1_ref, hp_ref, eid_ref, gcol_ref):
    half = D_MODEL // 2
    mix = (jnp.dot(mr_ref[...], wo_ref[:RET_W, :], preferred_element_type=F32)
           + jnp.dot(mm_ref[...], wo_ref[RET_W:, :], preferred_element_type=F32))
    x1 = x_ref[...] + g1_ref[...] * mix
    x1_ref[...] = x1
    y = x1 * lax.rsqrt(jnp.mean(x1 * x1, axis=-1, keepdims=True) + EPS) * ng_ref[...]
    h2 = (y * (1.0 + sc_ref[...]) + sh_ref[...]).astype(BF16)
    hi = lax.bitcast_convert_type(h2[:, :half].astype(F32), I32)
    lo = lax.bitcast_convert_type(h2[:, half:].astype(F32), I32)
    hp_ref[...] = (hi & jnp.int32(-65536)) | lax.shift_right_logical(lo, 16)

    logits = jnp.dot(h2, wr_ref[...], preferred_element_type=F32)
    lt = logits.T
    s_rows = [jax.nn.sigmoid(lt[e:e + 1, :]) for e in range(N_EXPERTS)]
    sel_rows = [s_rows[e] + br_ref[e:e + 1, :] for e in range(N_EXPERTS)]
    e0, e1, g0, g1 = _route_rows(s_rows, sel_rows)
    eid_ref[0:1, :] = e0
    eid_ref[1:2, :] = e1
    row = lax.broadcasted_iota(I32, (LANES, g0.shape[1]), 0)
    gcol_ref[...] = jnp.where(row == 0, g0, jnp.where(row == 1, g1, 0.0)).T


def _outproj_call(x, mix_ret, mix_mla, wo, mod3, ng, wr, br, n_ctx, seq_lat):
    n = x.shape[0]
    tm = OUT_TILE
    mrow = functools.partial(_mod_row, tile=tm, n_ctx=n_ctx, seq_lat=seq_lat)

    def mspec(chunk):
        return pl.BlockSpec((None, 1, D_MODEL), lambda i: (mrow(i), 0, chunk))

    return pl.pallas_call(
        _outproj_kernel,
        grid=(n // tm,),
        in_specs=[pl.BlockSpec((tm, D_MODEL), lambda i: (i, 0)),
                  pl.BlockSpec((tm, RET_W), lambda i: (i, 0)),
                  pl.BlockSpec((tm, H_MLA * D_V), lambda i: (i, 0)),
                  pl.BlockSpec((D_MODEL, D_MODEL), lambda i: (0, 0)),
                  mspec(2), mspec(4), mspec(3),
                  pl.BlockSpec((1, D_MODEL), lambda i: (0, 0)),
                  pl.BlockSpec((D_MODEL, LANES), lambda i: (0, 0)),
                  pl.BlockSpec((N_EXPERTS, 1), lambda i: (0, 0))],
        out_specs=[pl.BlockSpec((tm, D_MODEL), lambda i: (i, 0)),
                   pl.BlockSpec((tm, D_MODEL // 2), lambda i: (i, 0)),
                   pl.BlockSpec((2, tm), lambda i: (0, i)),
                   pl.BlockSpec((tm, LANES), lambda i: (i, 0))],
        out_shape=[jax.ShapeDtypeStruct((n, D_MODEL), F32),
                   jax.ShapeDtypeStruct((n, D_MODEL // 2), I32),
                   jax.ShapeDtypeStruct((2, n), I32),
                   jax.ShapeDtypeStruct((n, LANES), F32)],
        compiler_params=_params(("arbitrary",)),
        name="outproj_router",
    )(x, mix_ret, mix_mla, wo, mod3, mod3, mod3, ng, wr, br)


def _row_copy(src, dst, sem, s, d, rows=1):
    return pltpu.make_async_copy(src.at[pl.ds(s, rows)], dst.at[pl.ds(d, rows)], sem)


def _dispatch_kernel(meta_ref, dest_ref, src_ref, out_ref, sem, *, tile, nsteps):
    i = pl.program_id(0)
    base = i * tile

    def issue(t, carry):
        _row_copy(src_ref, out_ref, sem.at[0], base + t, dest_ref[0, t]).start()
        _row_copy(src_ref, out_ref, sem.at[0], base + t, dest_ref[1, t]).start()
        return carry

    lax.fori_loop(0, tile, issue, 0)

    def drain(t, carry):
        _row_copy(src_ref, out_ref, sem.at[0], 0, 0).wait()
        return carry

    lax.fori_loop(0, 2 * tile, drain, 0)

    @pl.when(i == nsteps - 1)
    def _():
        for e in range(N_EXPERTS):
            first = meta_ref[0, e]
            npad = meta_ref[1, e]

            def fill(r, carry):
                _row_copy(src_ref, out_ref, sem.at[1], 0, first + r).start()
                return carry

            lax.fori_loop(0, npad, fill, 0)

            def fdrain(r, carry):
                _row_copy(src_ref, out_ref, sem.at[1], 0, 0).wait()
                return carry

            lax.fori_loop(0, npad, fdrain, 0)


def _dispatch_call(meta, dest, hp, n_slots):
    n, wcols = hp.shape
    tile = GATHER_TILE
    nsteps = n // tile
    return pl.pallas_call(
        functools.partial(_dispatch_kernel, tile=tile, nsteps=nsteps),
        grid_spec=pltpu.PrefetchScalarGridSpec(
            num_scalar_prefetch=1, grid=(nsteps,),
            in_specs=[pl.BlockSpec((2, tile), lambda i, m: (0, i), memory_space=pltpu.SMEM),
                      pl.BlockSpec(memory_space=pl.ANY)],
            out_specs=pl.BlockSpec(memory_space=pl.ANY),
            scratch_shapes=[pltpu.SemaphoreType.DMA((2,))]),
        out_shape=jax.ShapeDtypeStruct((n_slots, wcols), I32),
        compiler_params=_params(("arbitrary",), disable_bounds_checks=True),
        name="moe_dispatch",
    )(meta, dest, hp)


def _expert_kernel(be_ref, nv_ref, x_ref, wg_ref, wu_ref, wd_ref, o_ref):
    half = D_MODEL // 2

    @pl.when(pl.program_id(0) < nv_ref[0])
    def _():
        w = x_ref[...]
        xa = lax.bitcast_convert_type(w & jnp.int32(-65536), F32).astype(BF16)
        xb = lax.bitcast_convert_type(lax.shift_left(w, 16), F32).astype(BF16)
        hg = (jnp.dot(xa, wg_ref[:half, :], preferred_element_type=F32)
              + jnp.dot(xb, wg_ref[half:, :], preferred_element_type=F32))
        hu = (jnp.dot(xa, wu_ref[:half, :], preferred_element_type=F32)
              + jnp.dot(xb, wu_ref[half:, :], preferred_element_type=F32))
        act = (_silu(hg) * hu).astype(BF16)
        o_ref[...] = jnp.dot(act, wd_ref[...], preferred_element_type=F32)


def _expert_call(block_e, n_valid, xs, wg, wu, wd):
    n_slots = xs.shape[0]
    tm = MOE_TILE
    nb = n_slots // tm

    def blk(b, be, nv):
        return jnp.minimum(b, nv[0] - 1)

    return pl.pallas_call(
        _expert_kernel,
        grid_spec=pltpu.PrefetchScalarGridSpec(
            num_scalar_prefetch=2, grid=(nb,),
            in_specs=[pl.BlockSpec((tm, D_MODEL // 2), lambda b, be, nv: (blk(b, be, nv), 0)),
                      pl.BlockSpec((None, D_MODEL, D_EXPERT), lambda b, be, nv: (be[blk(b, be, nv)], 0, 0)),
                      pl.BlockSpec((None, D_MODEL, D_EXPERT), lambda b, be, nv: (be[blk(b, be, nv)], 0, 0)),
                      pl.BlockSpec((None, D_EXPERT, D_MODEL), lambda b, be, nv: (be[blk(b, be, nv)], 0, 0))],
            out_specs=pl.BlockSpec((tm, D_MODEL), lambda b, be, nv: (blk(b, be, nv), 0))),
        out_shape=jax.ShapeDtypeStruct((n_slots, D_MODEL), F32),
        compiler_params=_params(("arbitrary",)),
        name="moe_experts",
    )(block_e, n_valid, xs, wg, wu, wd)


def _combine_kernel(dcur_ref, dnext_ref, o_ref, x_ref, gcol_ref, g2_ref, fg_ref, out_ref, buf, sem,
                    *, tile, nsteps, final):
    i = pl.program_id(0)
    slot = i % 2

    def issue(dref, s):
        def body(t, carry):
            _row_copy(o_ref, buf.at[s, 0], sem.at[s], dref[0, t], t).start()
            _row_copy(o_ref, buf.at[s, 1], sem.at[s], dref[1, t], t).start()
            return carry
        lax.fori_loop(0, tile, body, 0)

    @pl.when(i == 0)
    def _():
        issue(dcur_ref, 0)

    @pl.when(i + 1 < nsteps)
    def _():
        issue(dnext_ref, 1 - slot)

    def drain(t, carry):
        _row_copy(o_ref, buf.at[slot, 0], sem.at[slot], 0, 0).wait()
        return carry

    lax.fori_loop(0, 2 * tile, drain, 0)

    gc = gcol_ref[...]
    ffn = gc[:, 0:1] * buf[slot, 0] + gc[:, 1:2] * buf[slot, 1]
    x2 = x_ref[...] + g2_ref[...] * ffn
    if final:
        x2 = x2 * lax.rsqrt(jnp.mean(x2 * x2, axis=-1, keepdims=True) + EPS) * fg_ref[...]
    out_ref[...] = x2


def _combine_call(dest, o, x1, gcol, mod3, fg, n_ctx, seq_lat, final):
    n = x1.shape[0]
    tile = GATHER_TILE
    nsteps = n // tile
    mrow = functools.partial(_mod_row, tile=tile, n_ctx=n_ctx, seq_lat=seq_lat)
    return pl.pallas_call(
        functools.partial(_combine_kernel, tile=tile, nsteps=nsteps, final=final),
        grid=(nsteps,),
        in_specs=[pl.BlockSpec((2, tile), lambda i: (0, i), memory_space=pltpu.SMEM),
                  pl.BlockSpec((2, tile), lambda i: (0, jnp.minimum(i + 1, nsteps - 1)), memory_space=pltpu.SMEM),
                  pl.BlockSpec(memory_space=pl.ANY),
                  pl.BlockSpec((tile, D_MODEL), lambda i: (i, 0)),
                  pl.BlockSpec((tile, LANES), lambda i: (i, 0)),
                  pl.BlockSpec((None, 1, D_MODEL), lambda i: (mrow(i), 0, 5)),
                  pl.BlockSpec((1, D_MODEL), lambda i: (0, 0))],
        out_specs=pl.BlockSpec((tile, D_MODEL), lambda i: (i, 0)),
        out_shape=jax.ShapeDtypeStruct((n, D_MODEL), F32),
        scratch_shapes=[pltpu.VMEM((2, 2, tile, D_MODEL), F32), pltpu.SemaphoreType.DMA((2,))],
        compiler_params=_params(("arbitrary",), disable_bounds_checks=True),
        name="moe_combine",
    )(dest, dest, o, x1, gcol, mod3, fg)


def _axial_tables(n, dim):
    t = jnp.arange(n)
    rows = (t // GRID_W).astype(F32)
    cols = (t % GRID_W).astype(F32)
    nf = dim // 4
    inv = jnp.power(ROPE_BASE, -jnp.arange(nf, dtype=F32) / nf)
    ang = jnp.concatenate([rows[:, None] * inv, cols[:, None] * inv], axis=-1)
    return jnp.cos(ang), jnp.sin(ang)


def _rope_lane_tables(n, dim):
    cos, sin = _axial_tables(n, dim)
    c = jnp.concatenate([cos, cos], axis=-1)
    s = jnp.concatenate([-sin, sin], axis=-1)
    pad = LANES - dim
    if pad:
        c = jnp.pad(c, ((0, 0), (0, pad)))
        s = jnp.pad(s, ((0, 0), (0, pad)))
    return c, s


def _identity_rows(n, dim):
    c = jnp.pad(jnp.ones((n, dim), F32), ((0, 0), (0, LANES - dim)))
    return c, jnp.zeros((n, LANES), F32)


def _swap_halves_cols(w):
    half = w.shape[-1] // 2
    return jnp.concatenate([w[..., half:], w[..., :half]], axis=-1)


def _moe_plan(eid, n_slots):
    tm = MOE_TILE
    n = eid.shape[1]
    e = eid.reshape(-1)
    onehot = (e[:, None] == jnp.arange(N_EXPERTS, dtype=I32)[None, :]).astype(I32)
    csum = jnp.cumsum(onehot, axis=0)
    rank = jnp.sum(csum * onehot, axis=1) - 1
    counts = csum[-1]
    padded = (counts + tm - 1) // tm * tm
    pend = jnp.cumsum(padded)
    pstart = pend - padded
    dest = (jnp.sum(onehot * pstart[None, :], axis=1) + rank).reshape(2, n).astype(I32)
    n_valid = (pend[-1] // tm).astype(I32)
    blk_start = jnp.arange(n_slots // tm, dtype=I32) * tm
    block_e = jnp.minimum(jnp.sum((blk_start[:, None] >= pend[None, :]).astype(I32), axis=1), N_EXPERTS - 1)
    meta = jnp.stack([pstart + counts, padded - counts]).astype(I32)
    return dest, block_e.astype(I32), n_valid.reshape(1), meta


def kernel(x_prompt, x_sample, c, cache_ckv, cache_krope, state_ret, c_ctx, w_ada, b_ada, norm_attn, norm_ffn,
           w_in, ret_decay_logit, q_norm, kv_norm, w_uq, w_ukv, w_out, w_router, b_router, w_exp_gate, w_exp_up,
           w_exp_down, final_norm):
    batch, seq, d = x_prompt.shape
    dbatch, dseq, _ = x_sample.shape
    depth = w_ada.shape[0]
    past = cache_ckv.shape[2]
    n_ctx, n_lat = batch * seq, dbatch * dseq
    n = n_ctx + n_lat
    assert d == D_MODEL and dbatch + 1 <= 8
    assert n_ctx % dseq == 0 and dseq % ROW_TILE == 0 and n_ctx % ROW_TILE == 0 and seq % RET_CHUNK == 0
    n_slots = (2 * n // MOE_TILE + N_EXPERTS) * MOE_TILE

    x = jnp.concatenate([x_prompt.reshape(n_ctx, d), x_sample.reshape(n_lat, d)], axis=0)
    cond = jnp.concatenate([c_ctx[None, :], c, jnp.zeros((8 - 1 - dbatch, d), F32)], axis=0)
    mod = _ada_call(cond, w_ada, b_ada)

    cr, sr = _rope_lane_tables(dseq, D_RET)
    cm, sm = _rope_lane_tables(dseq, D_ROPE)
    one_q, zero_q = _identity_rows(ROW_TILE, D_ROPE)
    cos_q, sin_q = jnp.concatenate([one_q, cm], axis=0), jnp.concatenate([zero_q, sm], axis=0)
    one_p, zero_p = _identity_rows(past, D_ROPE)
    cos_k, sin_k = jnp.concatenate([one_p, cm], axis=0), jnp.concatenate([zero_p, sm], axis=0)
    one_c, zero_c = _identity_rows(seq, D_ROPE)
    eye = jnp.eye(LANES, dtype=BF16)

    wr = jnp.pad(w_router, ((0, 0), (0, LANES - N_EXPERTS))).astype(BF16)
    br = b_router.reshape(N_EXPERTS, 1).astype(F32)
    lg_all = jax.nn.log_sigmoid(ret_decay_logit.astype(F32))

    ckv_layers, krope_layers, state_layers = [], [], []
    for l in range(depth):
        mod3 = mod[l].reshape(8, 1, 6 * d)
        wl = w_in[l]
        cuts = 4 * RET_W + Q_RANK + KV_RANK
        w_in_p = jnp.concatenate(
            [wl, _swap_halves_cols(wl[:, cuts:]), jnp.zeros((d, IN_W - wl.shape[1] - D_ROPE), F32)],
            axis=1).astype(BF16)
        wq = w_uq[l].reshape(Q_RANK, H_MLA, D_NOPE + D_ROPE)
        w_uq_p = jnp.concatenate([wq, _swap_halves_cols(wq[..., D_NOPE:])], axis=-1)
        w_uq_p = w_uq_p.reshape(Q_RANK, H_MLA * Q_HEAD_W).astype(BF16)
        wkv = w_ukv[l].reshape(KV_RANK, H_MLA, D_NOPE + D_V)
        wkt = wkv[..., :D_NOPE].transpose(1, 2, 0).astype(BF16)
        wv = wkv[..., D_NOPE:].transpose(1, 0, 2).astype(BF16)
        wo = w_out[l].astype(BF16)
        wg, wu, wd = w_exp_gate[l].astype(BF16), w_exp_up[l].astype(BF16), w_exp_down[l].astype(BF16)

        qkvg, cqn, ckv, kr = _inproj_call(x, mod3, norm_attn[l][None, :], w_in_p, q_norm[l][None, :],
                                          kv_norm[l][None, :], n_ctx, dseq)
        ckv_layers.append(ckv[:n_ctx].reshape(batch, seq, KV_RANK))
        krope_layers.append(kr[:n_ctx, :D_ROPE].reshape(batch, seq, D_ROPE))

        mix_ret, s_ctx = _retention_call(lg_all[l], qkvg, batch, seq, 0, H_RET, emit_state=True)
        (mix_ret,) = _retention_call(lg_all[l], qkvg, dbatch, dseq, n_ctx // dseq, 1, s0=state_ret[:, l],
                                     tables=(cr, sr), prev=mix_ret)
        state_layers.append(s_ctx)

        q = _qproj_call(cqn, w_uq_p, cos_q, sin_q, n_ctx, dseq)
        kt_c, v_c = _kvproj_call(ckv[:n_ctx].reshape(batch, seq, KV_RANK).astype(BF16),
                                 kr[:n_ctx].reshape(batch, seq, LANES), one_c, zero_c, wkt, wv, eye, H_MLA)
        keys_ckv = jnp.concatenate([cache_ckv[:, l], ckv[n_ctx:].reshape(dbatch, dseq, KV_RANK)], axis=1).astype(BF16)
        keys_kr = jnp.concatenate([jnp.pad(cache_krope[:, l], ((0, 0), (0, 0), (0, LANES - D_ROPE))),
                                   kr[n_ctx:].reshape(dbatch, dseq, LANES)], axis=1)
        kt_l, v_l = _kvproj_call(keys_ckv, keys_kr, cos_k, sin_k, wkt, wv, eye, 2)
        mix_mla = _attn_call(q, kt_c, v_c, seq, 0, seq, H_MLA)
        mix_mla = _attn_call(q, kt_l, v_l, dseq, n_ctx, Q_TILE, 1, prev=mix_mla)

        x1, hp, eid, gcol = _outproj_call(x, mix_ret, mix_mla, wo, mod3, norm_ffn[l][None, :], wr, br, n_ctx, dseq)

        dest, block_e, n_valid, meta = _moe_plan(eid, n_slots)
        xs = _dispatch_call(meta, dest, hp, n_slots)
        o = _expert_call(block_e, n_valid, xs, wg, wu, wd)
        x = _combine_call(dest, o, x1, gcol, mod3, final_norm[None, :], n_ctx, dseq, l == depth - 1)

    y_prompt = x[:n_ctx].reshape(batch, seq, d)
    y_sample = x[n_ctx:].reshape(dbatch, dseq, d)
    return (y_prompt, y_sample, jnp.stack(ckv_layers, axis=1), jnp.stack(krope_layers, axis=1),
            jnp.stack(state_layers, axis=1))
```

```python
import functools
import math

import jax
import jax.numpy as jnp
from jax import lax
from jax.experimental import pallas as pl
from jax.experimental.pallas import tpu as pltpu

F32 = jnp.float32
BF16 = jnp.bfloat16
I32 = jnp.int32

D_MODEL = 2048
GRID_W = 64
H_RET = 8
D_RET = 128
RET_W = H_RET * D_RET
RET_CHUNK = 128
H_MLA = 8
D_NOPE = 128
D_ROPE = 64
D_V = 128
Q_RANK = 512
KV_RANK = 256
N_EXPERTS = 16
N_GROUPS = 4
EXPERTS_PER_GROUP = N_EXPERTS // N_GROUPS
D_EXPERT = 1024
ROPE_BASE = 10000.0
EPS = 1e-6

LANES = 128
Q_HEAD_W = 2 * LANES
LAT_W = 1024
IN_W = 4 * RET_W + LAT_W
ROW_TILE = 512
OUT_TILE = 256
Q_TILE = 256
KEY_CHUNK = 512
MOE_TILE = 256
GATHER_TILE = 256
DMA_UNROLL = 8
VMEM_LIMIT = 56 * 1024 * 1024


def _params(sem, **kw):
    return pltpu.CompilerParams(dimension_semantics=sem, vmem_limit_bytes=VMEM_LIMIT, **kw)


def _silu(x):
    return x * jax.nn.sigmoid(x)


def _mod_row(i, tile, n_ctx, seq_lat):
    r0 = i * tile
    return jnp.where(r0 < n_ctx, 0, 1 + (r0 - n_ctx) // seq_lat)


def _pos_block(i, tile, n_ctx, seq_lat):
    r0 = i * tile
    return jnp.where(r0 < n_ctx, 0, 1 + ((r0 - n_ctx) % seq_lat) // tile)


def _ada_kernel(c_ref, w_ref, b_ref, o_ref):
    a = _silu(c_ref[...]).astype(BF16)
    o_ref[...] = jnp.dot(a, w_ref[...].astype(BF16), preferred_element_type=F32) + b_ref[...]


def _ada_call(cond, w_ada, b_ada):
    depth, d, n6 = w_ada.shape
    tn = 1536
    return pl.pallas_call(
        _ada_kernel,
        grid=(depth, n6 // tn),
        in_specs=[pl.BlockSpec((8, d), lambda l, j: (0, 0)),
                  pl.BlockSpec((None, d, tn), lambda l, j: (l, 0, j)),
                  pl.BlockSpec((None, 1, tn), lambda l, j: (l, 0, j))],
        out_specs=pl.BlockSpec((None, 8, tn), lambda l, j: (l, 0, j)),
        out_shape=jax.ShapeDtypeStruct((depth, 8, n6), F32),
        compiler_params=_params(("arbitrary", "arbitrary")),
        name="adaln",
    )(cond, w_ada, b_ada.reshape(depth, 1, n6))


def _inproj_kernel(x_ref, sh_ref, sc_ref, ng_ref, w_ref, qg_ref, kvg_ref,
                   qkvg_ref, cq_ref, ckv_ref, kr_ref, h_scr, *, nj):
    j = pl.program_id(1)

    @pl.when(j == 0)
    def _():
        x = x_ref[...]
        y = x * lax.rsqrt(jnp.mean(x * x, axis=-1, keepdims=True) + EPS) * ng_ref[...]
        h_scr[...] = (y * (1.0 + sc_ref[...]) + sh_ref[...]).astype(BF16)

    @pl.when(j < nj - 1)
    def _():
        qkvg_ref[...] = jnp.dot(h_scr[...], w_ref[...], preferred_element_type=F32).astype(BF16)

    @pl.when(j == nj - 1)
    def _():
        acc = jnp.dot(h_scr[...], w_ref[...], preferred_element_type=F32)
        cq = acc[:, :Q_RANK]
        cq = cq * lax.rsqrt(jnp.mean(cq * cq, axis=-1, keepdims=True) + EPS) * qg_ref[...]
        cq_ref[...] = cq.astype(BF16)
        ckv = acc[:, Q_RANK:Q_RANK + KV_RANK]
        ckv_ref[...] = ckv * lax.rsqrt(jnp.mean(ckv * ckv, axis=-1, keepdims=True) + EPS) * kvg_ref[...]
        kr_ref[...] = acc[:, Q_RANK + KV_RANK:Q_RANK + KV_RANK + LANES]


def _inproj_call(x, mod3, ng, w_in_p, qg, kvg, n_ctx, seq_lat):
    n = x.shape[0]
    tm = ROW_TILE
    nj = IN_W // LAT_W
    mrow = functools.partial(_mod_row, tile=tm, n_ctx=n_ctx, seq_lat=seq_lat)
    return pl.pallas_call(
        functools.partial(_inproj_kernel, nj=nj),
        grid=(n // tm, nj),
        in_specs=[pl.BlockSpec((tm, D_MODEL), lambda i, j: (i, 0)),
                  pl.BlockSpec((None, 1, D_MODEL), lambda i, j: (mrow(i), 0, 0)),
                  pl.BlockSpec((None, 1, D_MODEL), lambda i, j: (mrow(i), 0, 1)),
                  pl.BlockSpec((1, D_MODEL), lambda i, j: (0, 0)),
                  pl.BlockSpec((D_MODEL, LAT_W), lambda i, j: (0, j)),
                  pl.BlockSpec((1, Q_RANK), lambda i, j: (0, 0)),
                  pl.BlockSpec((1, KV_RANK), lambda i, j: (0, 0))],
        out_specs=[pl.BlockSpec((tm, LAT_W), lambda i, j: (i, jnp.minimum(j, nj - 2))),
                   pl.BlockSpec((tm, Q_RANK), lambda i, j: (i, 0)),
                   pl.BlockSpec((tm, KV_RANK), lambda i, j: (i, 0)),
                   pl.BlockSpec((tm, LANES), lambda i, j: (i, 0))],
        out_shape=[jax.ShapeDtypeStruct((n, 4 * RET_W), BF16),
                   jax.ShapeDtypeStruct((n, Q_RANK), BF16),
                   jax.ShapeDtypeStruct((n, KV_RANK), F32),
                   jax.ShapeDtypeStruct((n, LANES), F32)],
        scratch_shapes=[pltpu.VMEM((tm, D_MODEL), BF16)],
        compiler_params=_params(("arbitrary", "arbitrary")),
        name="inproj",
    )(x, mod3, mod3, ng, w_in_p, qg, kvg)


def _retention_kernel(lg_ref, *refs, n, hb, unroll, rope, has_s0, emit_state):
    it = iter(refs)
    q_ref, k_ref, v_ref, g_ref = next(it), next(it), next(it), next(it)
    s0_ref = next(it) if has_s0 else None
    cos_ref, sin_ref = (next(it), next(it)) if rope else (None, None)
    o_ref = next(it)
    sfin_ref = next(it) if emit_state else None
    kvf_scr, kvb_scr, kt_scr, dm_scr = next(it), next(it), next(it), next(it)

    C = RET_CHUNK
    nc = n // C
    hblk = pl.program_id(1)
    diff = (lax.broadcasted_iota(I32, (C, C), 0) - lax.broadcasted_iota(I32, (C, C), 1)).astype(F32)
    col = lax.broadcasted_iota(I32, (C, 1), 0).astype(F32)
    lane = lax.broadcasted_iota(I32, (1, C), 1).astype(F32)
    k_scale = D_RET ** -0.5

    xi_f, xi_b, zeta_f, zeta_b, g_f, g_b = [], [], [], [], [], []
    for hh in range(hb):
        lgf = lg_ref[0, hblk * hb + hh]
        lgb = lg_ref[1, hblk * hb + hh]
        dm_scr[hh] = (jnp.where(diff >= 0, jnp.exp(jnp.maximum(diff, 0.0) * lgf), 0.0)
                      + jnp.where(diff <= 0, jnp.exp(jnp.maximum(-diff, 0.0) * lgb), 0.0))
        xi_f.append(jnp.exp((col + 1.0) * lgf))
        xi_b.append(jnp.exp((C - col) * lgb))
        zeta_f.append(jnp.exp((C - 1.0 - lane) * lgf))
        zeta_b.append(jnp.exp(lane * lgb))
        g_f.append(jnp.exp(jnp.full((1, C), float(C), F32) * lgf))
        g_b.append(jnp.exp(jnp.full((1, C), float(C), F32) * lgb))

    def rows(c):
        return pl.ds(pl.multiple_of(c * C, C), C)

    def cols(hh):
        return slice(hh * C, (hh + 1) * C)

    def rot(x, c):
        if not rope:
            return x
        return x * cos_ref[rows(c), :] + pltpu.roll(x, C // 2, 1) * sin_ref[rows(c), :]

    def phase_a(c, carry):
        for hh in range(hb):
            kt = rot(k_ref[rows(c), cols(hh)].astype(F32) * k_scale, c).T
            kt_scr[hh, c] = kt.astype(BF16)
            vc = v_ref[rows(c), cols(hh)]
            kvf_scr[hh, c] = jnp.dot((kt * zeta_f[hh]).astype(BF16), vc, preferred_element_type=F32)
            kvb_scr[hh, c] = jnp.dot((kt * zeta_b[hh]).astype(BF16), vc, preferred_element_type=F32)
        return carry

    lax.fori_loop(0, nc, phase_a, 0, unroll=unroll)

    for hh in range(hb):
        def scan(i, carry, hh=hh):
            sf, sb = carry
            cb = nc - 1 - i
            upd_f = kvf_scr[hh, i]
            kvf_scr[hh, i] = sf
            upd_b = kvb_scr[hh, cb]
            kvb_scr[hh, cb] = sb
            return g_f[hh] * sf + upd_f, g_b[hh] * sb + upd_b

        if has_s0:
            init = (s0_ref[0, hh], s0_ref[1, hh])
        else:
            init = (jnp.zeros((C, C), F32), jnp.zeros((C, C), F32))
        s_f, s_b = lax.fori_loop(0, nc, scan, init)
        if emit_state:
            sfin_ref[0, hh] = s_f
            sfin_ref[1, hh] = s_b

    def phase_c(c, carry):
        for hh in range(hb):
            qc = rot(q_ref[rows(c), cols(hh)].astype(F32), c)
            inner = jnp.dot(qc.astype(BF16), kt_scr[hh, c], preferred_element_type=F32) * dm_scr[hh]
            o = (jnp.dot(inner.astype(BF16), v_ref[rows(c), cols(hh)], preferred_element_type=F32)
                 + jnp.dot((qc * xi_f[hh]).astype(BF16), kvf_scr[hh, c].astype(BF16), preferred_element_type=F32)
                 + jnp.dot((qc * xi_b[hh]).astype(BF16), kvb_scr[hh, c].astype(BF16), preferred_element_type=F32))
            dev = o - jnp.mean(o, axis=-1, keepdims=True)
            on = dev * lax.rsqrt(jnp.mean(dev * dev, axis=-1, keepdims=True) + EPS)
            gate = g_ref[rows(c), cols(hh)].astype(F32)
            o_ref[rows(c), cols(hh)] = (on * _silu(gate)).astype(BF16)
        return carry

    lax.fori_loop(0, nc, phase_c, 0, unroll=unroll)


def _retention_call(lg2, qkvg, batch, n, row_blk0, hb, unroll, *, s0=None, tables=None, emit_state=False):
    hblks = H_RET // hb
    w = hb * D_RET
    nc = n // RET_CHUNK

    def qspec(part):
        return pl.BlockSpec((n, w), lambda b, h, lg: (row_blk0 + b, part * hblks + h))

    in_specs = [qspec(0), qspec(1), qspec(2), qspec(3)]
    args = [qkvg, qkvg, qkvg, qkvg]
    if s0 is not None:
        in_specs.append(pl.BlockSpec((None, 2, hb, D_RET, D_RET), lambda b, h, lg: (b, 0, h, 0, 0)))
        args.append(s0)
    if tables is not None:
        in_specs += [pl.BlockSpec((n, D_RET), lambda b, h, lg: (0, 0))] * 2
        args += list(tables)
    out_specs = [pl.BlockSpec((n, w), lambda b, h, lg: (b, h))]
    out_shape = [jax.ShapeDtypeStruct((batch * n, RET_W), BF16)]
    if emit_state:
        out_specs.append(pl.BlockSpec((None, 2, hb, D_RET, D_RET), lambda b, h, lg: (b, 0, h, 0, 0)))
        out_shape.append(jax.ShapeDtypeStruct((batch, 2, H_RET, D_RET, D_RET), F32))
    kern = functools.partial(_retention_kernel, n=n, hb=hb, unroll=unroll, rope=tables is not None,
                             has_s0=s0 is not None, emit_state=emit_state)
    return pl.pallas_call(
        kern,
        grid_spec=pltpu.PrefetchScalarGridSpec(
            num_scalar_prefetch=1, grid=(batch, hblks), in_specs=in_specs, out_specs=out_specs,
            scratch_shapes=[pltpu.VMEM((hb, nc, D_RET, D_RET), F32), pltpu.VMEM((hb, nc, D_RET, D_RET), F32),
                            pltpu.VMEM((hb, nc, D_RET, D_RET), BF16), pltpu.VMEM((hb, D_RET, D_RET), F32)]),
        out_shape=out_shape,
        compiler_params=_params(("arbitrary", "arbitrary")),
        name="retention",
    )(lg2, *args)


def _qproj_kernel(cq_ref, w_ref, cos_ref, sin_ref, q_ref):
    scale = (D_NOPE + D_ROPE) ** -0.5 * math.log2(math.e)
    acc = jnp.dot(cq_ref[...], w_ref[...], preferred_element_type=F32)
    for h in range(H_MLA):
        base = h * Q_HEAD_W
        q_ref[:, base:base + LANES] = (acc[:, base:base + LANES] * scale).astype(BF16)
        tail = acc[:, base + LANES:base + Q_HEAD_W]
        rot = tail * cos_ref[...] + pltpu.roll(tail, D_ROPE, 1) * sin_ref[...]
        q_ref[:, base + LANES:base + Q_HEAD_W] = (rot * scale).astype(BF16)


def _qproj_call(cqn, w_uq_p, cos_q, sin_q, n_ctx, seq_lat):
    n = cqn.shape[0]
    tm = ROW_TILE
    pblk = functools.partial(_pos_block, tile=tm, n_ctx=n_ctx, seq_lat=seq_lat)
    return pl.pallas_call(
        _qproj_kernel,
        grid=(n // tm,),
        in_specs=[pl.BlockSpec((tm, Q_RANK), lambda i: (i, 0)),
                  pl.BlockSpec((Q_RANK, H_MLA * Q_HEAD_W), lambda i: (0, 0)),
                  pl.BlockSpec((tm, LANES), lambda i: (pblk(i), 0)),
                  pl.BlockSpec((tm, LANES), lambda i: (pblk(i), 0))],
        out_specs=pl.BlockSpec((tm, H_MLA * Q_HEAD_W), lambda i: (i, 0)),
        out_shape=jax.ShapeDtypeStruct((n, H_MLA * Q_HEAD_W), BF16),
        compiler_params=_params(("arbitrary",)),
        name="qproj",
    )(cqn, w_uq_p, cos_q, sin_q)


def _kvproj_kernel(ckv_ref, kr_ref, cos_ref, sin_ref, wkt_ref, wv_ref, eye_ref, kt_ref, v_ref, *, hb):
    nt = (((1,), (1,)), ((), ()))
    keys = ckv_ref[...]
    kr = kr_ref[...]
    rot = (kr * cos_ref[...] + pltpu.roll(kr, D_ROPE, 1) * sin_ref[...]).astype(BF16)
    rope_t = lax.dot_general(eye_ref[...], rot, nt, preferred_element_type=F32).astype(BF16)
    for hh in range(hb):
        kt_ref[hh, :D_NOPE, :] = lax.dot_general(wkt_ref[hh], keys, nt, preferred_element_type=F32).astype(BF16)
        kt_ref[hh, D_NOPE:, :] = rope_t
        v_ref[hh] = jnp.dot(keys, wv_ref[hh], preferred_element_type=F32).astype(BF16)


def _kvproj_call(keys_ckv, keys_kr, cos_k, sin_k, wkt, wv, eye, hb):
    batch, nk, _ = keys_ckv.shape
    hblks = H_MLA // hb
    return pl.pallas_call(
        functools.partial(_kvproj_kernel, hb=hb),
        grid=(batch, hblks),
        in_specs=[pl.BlockSpec((None, nk, KV_RANK), lambda b, h: (b, 0, 0)),
                  pl.BlockSpec((None, nk, LANES), lambda b, h: (b, 0, 0)),
                  pl.BlockSpec((nk, LANES), lambda b, h: (0, 0)),
                  pl.BlockSpec((nk, LANES), lambda b, h: (0, 0)),
                  pl.BlockSpec((hb, D_NOPE, KV_RANK), lambda b, h: (h, 0, 0)),
                  pl.BlockSpec((hb, KV_RANK, D_V), lambda b, h: (h, 0, 0)),
                  pl.BlockSpec((LANES, LANES), lambda b, h: (0, 0))],
        out_specs=[pl.BlockSpec((None, hb, Q_HEAD_W, nk), lambda b, h: (b, h, 0, 0)),
                   pl.BlockSpec((None, hb, nk, D_V), lambda b, h: (b, h, 0, 0))],
        out_shape=[jax.ShapeDtypeStruct((batch, H_MLA, Q_HEAD_W, nk), BF16),
                   jax.ShapeDtypeStruct((batch, H_MLA, nk, D_V), BF16)],
        compiler_params=_params(("arbitrary", "arbitrary")),
        name="kvproj",
    )(keys_ckv, keys_kr, cos_k, sin_k, wkt, wv, eye)


def _attn_kernel(q_ref, kt_ref, v_ref, o_ref, *, hb):
    nk = kt_ref.shape[-1]
    kc = KEY_CHUNK if nk % KEY_CHUNK == 0 else nk
    for hh in range(hb):
        q = q_ref[:, hh * Q_HEAD_W:(hh + 1) * Q_HEAD_W]
        s = jnp.dot(q, kt_ref[hh], preferred_element_type=F32)
        m = jnp.max(s, axis=-1, keepdims=True)
        den = acc = None
        for j in range(nk // kc):
            p = jnp.exp2(s[:, j * kc:(j + 1) * kc] - m)
            psum = p[:, :LANES]
            for g in range(1, kc // LANES):
                psum = psum + p[:, g * LANES:(g + 1) * LANES]
            pv = jnp.dot(p.astype(BF16), v_ref[hh, j * kc:(j + 1) * kc, :], preferred_element_type=F32)
            den, acc = (psum, pv) if j == 0 else (den + psum, acc + pv)
        o = acc / jnp.sum(den, axis=-1, keepdims=True)
        o_ref[:, hh * D_V:(hh + 1) * D_V] = o.astype(BF16)


def _attn_call(q, kt, v, n, row0, tq, hb):
    batch, _, _, nk = kt.shape
    hblks = H_MLA // hb
    nq = n // tq
    blk0 = row0 // tq
    return pl.pallas_call(
        functools.partial(_attn_kernel, hb=hb),
        grid=(batch, hblks, nq),
        in_specs=[pl.BlockSpec((tq, hb * Q_HEAD_W), lambda b, h, i: (blk0 + b * nq + i, h)),
                  pl.BlockSpec((None, hb, Q_HEAD_W, nk), lambda b, h, i: (b, h, 0, 0)),
                  pl.BlockSpec((None, hb, nk, D_V), lambda b, h, i: (b, h, 0, 0))],
        out_specs=pl.BlockSpec((tq, hb * D_V), lambda b, h, i: (b * nq + i, h)),
        out_shape=jax.ShapeDtypeStruct((batch * n, H_MLA * D_V), BF16),
        compiler_params=_params(("arbitrary", "arbitrary", "arbitrary")),
        name="mla_attention",
    )(q, kt, v)


def _route_rows(s_rows, sel_rows):
    epg = EXPERTS_PER_GROUP
    best_g = best_v = None
    for g in range(N_GROUPS):
        v = sel_rows[g * epg:(g + 1) * epg]
        top2 = None
        for a in range(epg):
            for b in range(a + 1, epg):
                pair = v[a] + v[b]
                top2 = pair if top2 is None else jnp.maximum(top2, pair)
        if g == 0:
            best_v, best_g = top2, jnp.zeros(top2.shape, I32)
        else:
            upd = top2 > best_v
            best_v = jnp.where(upd, top2, best_v)
            best_g = jnp.where(upd, g, best_g)

    def pick(rows_, i):
        out = rows_[i]
        for g in range(1, N_GROUPS):
            out = jnp.where(best_g == g, rows_[g * epg + i], out)
        return out

    w = [pick(sel_rows, i) for i in range(epg)]
    sv = [pick(s_rows, i) for i in range(epg)]
    m0, i0, s0 = w[0], jnp.zeros(w[0].shape, I32), sv[0]
    for i in range(1, epg):
        upd = w[i] > m0
        m0, i0, s0 = jnp.where(upd, w[i], m0), jnp.where(upd, i, i0), jnp.where(upd, sv[i], s0)
    m1 = jnp.full(w[0].shape, -jnp.inf, F32)
    i1, s1 = jnp.zeros(w[0].shape, I32), sv[0]
    for i in range(epg):
        upd = (i0 != i) & (w[i] > m1)
        m1, i1, s1 = jnp.where(upd, w[i], m1), jnp.where(upd, i, i1), jnp.where(upd, sv[i], s1)
    tot = s0 + s1
    return best_g * epg + i0, best_g * epg + i1, s0 / tot, s1 / tot


def _outproj_kernel(x_ref, mrc_ref, mrl_ref, mmc_ref, mml_ref, wo_ref, g1_ref, sc_ref, sh_ref, ng_ref, wr_ref,
                    br_ref, x1_ref, hp_ref, eid_ref, gcol_ref, *, ctx_tiles):
    half = D_MODEL // 2
    is_ctx = pl.program_id(0) < ctx_tiles
    mr = jnp.where(is_ctx, mrc_ref[...], mrl_ref[...])
    mm = jnp.where(is_ctx, mmc_ref[...], mml_ref[...])
    mix = (jnp.dot(mr, wo_ref[:RET_W, :], preferred_element_type=F32)
           + jnp.dot(mm, wo_ref[RET_W:, :], preferred_element_type=F32))
    x1 = x_ref[...] + g1_ref[...] * mix
    x1_ref[...] = x1
    y = x1 * lax.rsqrt(jnp.mean(x1 * x1, axis=-1, keepdims=True) + EPS) * ng_ref[...]
    h2 = (y * (1.0 + sc_ref[...]) + sh_ref[...]).astype(BF16)
    hi = lax.bitcast_convert_type(h2[:, :half].astype(F32), I32)
    lo = lax.bitcast_convert_type(h2[:, half:].astype(F32), I32)
    hp_ref[...] = (hi & jnp.int32(-65536)) | lax.shift_right_logical(lo, 16)

    logits = jnp.dot(h2, wr_ref[...], preferred_element_type=F32)
    lt = logits.T
    s_rows = [jax.nn.sigmoid(lt[e:e + 1, :]) for e in range(N_EXPERTS)]
    sel_rows = [s_rows[e] + br_ref[e:e + 1, :] for e in range(N_EXPERTS)]
    e0, e1, g0, g1 = _route_rows(s_rows, sel_rows)
    eid_ref[0:1, :] = e0
    eid_ref[1:2, :] = e1
    row = lax.broadcasted_iota(I32, (LANES, g0.shape[1]), 0)
    gcol_ref[...] = jnp.where(row == 0, g0, jnp.where(row == 1, g1, 0.0)).T


def _outproj_call(x, mix_ret, mix_mla, wo, mod3, ng, wr, br, n_ctx, seq_lat):
    n = x.shape[0]
    tm = OUT_TILE
    ctx_tiles = n_ctx // tm
    mrow = functools.partial(_mod_row, tile=tm, n_ctx=n_ctx, seq_lat=seq_lat)

    def mspec(chunk):
        return pl.BlockSpec((None, 1, D_MODEL), lambda i: (mrow(i), 0, chunk))

    def ctx_spec(width):
        return pl.BlockSpec((tm, width), lambda i: (jnp.minimum(i, ctx_tiles - 1), 0))

    def lat_spec(width):
        return pl.BlockSpec((tm, width), lambda i: (jnp.maximum(i - ctx_tiles, 0), 0))

    return pl.pallas_call(
        functools.partial(_outproj_kernel, ctx_tiles=ctx_tiles),
        grid=(n // tm,),
        in_specs=[pl.BlockSpec((tm, D_MODEL), lambda i: (i, 0)),
                  ctx_spec(RET_W), lat_spec(RET_W), ctx_spec(H_MLA * D_V), lat_spec(H_MLA * D_V),
                  pl.BlockSpec((D_MODEL, D_MODEL), lambda i: (0, 0)),
                  mspec(2), mspec(4), mspec(3),
                  pl.BlockSpec((1, D_MODEL), lambda i: (0, 0)),
                  pl.BlockSpec((D_MODEL, LANES), lambda i: (0, 0)),
                  pl.BlockSpec((N_EXPERTS, 1), lambda i: (0, 0))],
        out_specs=[pl.BlockSpec((tm, D_MODEL), lambda i: (i, 0)),
                   pl.BlockSpec((tm, D_MODEL // 2), lambda i: (i, 0)),
                   pl.BlockSpec((2, tm), lambda i: (0, i)),
                   pl.BlockSpec((tm, LANES), lambda i: (i, 0))],
        out_shape=[jax.ShapeDtypeStruct((n, D_MODEL), F32),
                   jax.ShapeDtypeStruct((n, D_MODEL // 2), I32),
                   jax.ShapeDtypeStruct((2, n), I32),
                   jax.ShapeDtypeStruct((n, LANES), F32)],
        compiler_params=_params(("arbitrary",)),
        name="outproj_router",
    )(x, mix_ret[0], mix_ret[1], mix_mla[0], mix_mla[1], wo, mod3, mod3, mod3, ng, wr, br)


def _dispatch_kernel(meta_ref, dest_ref, src_ref, out_ref, sem, *, tile, nsteps):
    def row_to(t, slot, s):
        return pltpu.make_async_copy(src_ref.at[pl.ds(t, 1)], out_ref.at[pl.ds(slot, 1)], s)

    def issue(t, carry):
        row_to(t, dest_ref[0, t], sem.at[0]).start()
        row_to(t, dest_ref[1, t], sem.at[0]).start()
        return carry

    lax.fori_loop(0, tile, issue, 0, unroll=DMA_UNROLL)
    for _ in range(2):
        pltpu.make_async_copy(src_ref, out_ref.at[pl.ds(0, tile)], sem.at[0]).wait()

    @pl.when(pl.program_id(0) == nsteps - 1)
    def _():
        for e in range(N_EXPERTS):
            first = meta_ref[0, e]
            npad = meta_ref[1, e]

            def fill(r, carry):
                row_to(0, first + r, sem.at[1]).start()
                return carry

            lax.fori_loop(0, npad, fill, 0)

            def fdrain(r, carry):
                row_to(0, 0, sem.at[1]).wait()
                return carry

            lax.fori_loop(0, npad, fdrain, 0)


def _dispatch_call(meta, dest, hp, n_slots):
    n, wcols = hp.shape
    tile = GATHER_TILE
    nsteps = n // tile
    return pl.pallas_call(
        functools.partial(_dispatch_kernel, tile=tile, nsteps=nsteps),
        grid_spec=pltpu.PrefetchScalarGridSpec(
            num_scalar_prefetch=1, grid=(nsteps,),
            in_specs=[pl.BlockSpec((2, tile), lambda i, m: (0, i), memory_space=pltpu.SMEM),
                      pl.BlockSpec((tile, wcols), lambda i, m: (i, 0))],
            out_specs=pl.BlockSpec(memory_space=pl.ANY),
            scratch_shapes=[pltpu.SemaphoreType.DMA((2,))]),
        out_shape=jax.ShapeDtypeStruct((n_slots, wcols), I32),
        compiler_params=_params(("arbitrary",), disable_bounds_checks=True),
        name="moe_dispatch",
    )(meta, dest, hp)


def _expert_kernel(be_ref, nv_ref, x_ref, wg_ref, wu_ref, wd_ref, o_ref):
    half = D_MODEL // 2

    @pl.when(pl.program_id(0) < nv_ref[0])
    def _():
        w = x_ref[...]
        xa = lax.bitcast_convert_type(w & jnp.int32(-65536), F32).astype(BF16)
        xb = lax.bitcast_convert_type(lax.shift_left(w, 16), F32).astype(BF16)
        hg = (jnp.dot(xa, wg_ref[:half, :], preferred_element_type=F32)
              + jnp.dot(xb, wg_ref[half:, :], preferred_element_type=F32))
        hu = (jnp.dot(xa, wu_ref[:half, :], preferred_element_type=F32)
              + jnp.dot(xb, wu_ref[half:, :], preferred_element_type=F32))
        act = (_silu(hg) * hu).astype(BF16)
        o_ref[...] = jnp.dot(act, wd_ref[...], preferred_element_type=F32)


def _expert_call(block_e, n_valid, xs, wg, wu, wd):
    n_slots = xs.shape[0]
    tm = MOE_TILE
    nb = n_slots // tm

    def blk(b, be, nv):
        return jnp.minimum(b, nv[0] - 1)

    return pl.pallas_call(
        _expert_kernel,
        grid_spec=pltpu.PrefetchScalarGridSpec(
            num_scalar_prefetch=2, grid=(nb,),
            in_specs=[pl.BlockSpec((tm, D_MODEL // 2), lambda b, be, nv: (blk(b, be, nv), 0)),
                      pl.BlockSpec((None, D_MODEL, D_EXPERT), lambda b, be, nv: (be[blk(b, be, nv)], 0, 0)),
                      pl.BlockSpec((None, D_MODEL, D_EXPERT), lambda b, be, nv: (be[blk(b, be, nv)], 0, 0)),
                      pl.BlockSpec((None, D_EXPERT, D_MODEL), lambda b, be, nv: (be[blk(b, be, nv)], 0, 0))],
            out_specs=pl.BlockSpec((tm, D_MODEL), lambda b, be, nv: (blk(b, be, nv), 0))),
        out_shape=jax.ShapeDtypeStruct((n_slots, D_MODEL), F32),
        compiler_params=_params(("arbitrary",)),
        name="moe_experts",
    )(block_e, n_valid, xs, wg, wu, wd)


def _combine_kernel(dcur_ref, dnext_ref, o_ref, x_ref, gcol_ref, g2_ref, fg_ref, *rest,
                    tile, nsteps, ctx_tiles, final):
    if final:
        yc_ref, yl_ref, buf, sem = rest
    else:
        out_ref, buf, sem = rest
    i = pl.program_id(0)
    slot = i % 2

    def issue(dref, s):
        def body(t, carry):
            for k in range(2):
                pltpu.make_async_copy(o_ref.at[pl.ds(dref[k, t], 1)], buf.at[s, k, pl.ds(t, 1)], sem.at[s]).start()
            return carry
        lax.fori_loop(0, tile, body, 0, unroll=DMA_UNROLL)

    @pl.when(i == 0)
    def _():
        issue(dcur_ref, 0)

    @pl.when(i + 1 < nsteps)
    def _():
        issue(dnext_ref, 1 - slot)

    for k in range(2):
        pltpu.make_async_copy(o_ref.at[pl.ds(0, tile)], buf.at[slot, k], sem.at[slot]).wait()

    gc = gcol_ref[...]
    ffn = gc[:, 0:1] * buf[slot, 0] + gc[:, 1:2] * buf[slot, 1]
    x2 = x_ref[...] + g2_ref[...] * ffn
    if not final:
        out_ref[...] = x2
    else:
        y = x2 * lax.rsqrt(jnp.mean(x2 * x2, axis=-1, keepdims=True) + EPS) * fg_ref[...]

        @pl.when(i < ctx_tiles)
        def _():
            yc_ref[...] = y

        @pl.when(i >= ctx_tiles)
        def _():
            yl_ref[...] = y


def _combine_call(dest, o, x1, gcol, mod3, fg, n_ctx, seq_lat, final):
    n = x1.shape[0]
    tile = GATHER_TILE
    nsteps = n // tile
    ctx_tiles = n_ctx // tile
    mrow = functools.partial(_mod_row, tile=tile, n_ctx=n_ctx, seq_lat=seq_lat)
    if final:
        out_specs = [pl.BlockSpec((tile, D_MODEL), lambda i: (jnp.minimum(i, ctx_tiles - 1), 0)),
                     pl.BlockSpec((tile, D_MODEL), lambda i: (jnp.maximum(i - ctx_tiles, 0), 0))]
        out_shape = [jax.ShapeDtypeStruct((n_ctx, D_MODEL), F32), jax.ShapeDtypeStruct((n - n_ctx, D_MODEL), F32)]
    else:
        out_specs = pl.BlockSpec((tile, D_MODEL), lambda i: (i, 0))
        out_shape = jax.ShapeDtypeStruct((n, D_MODEL), F32)
    return pl.pallas_call(
        functools.partial(_combine_kernel, tile=tile, nsteps=nsteps, ctx_tiles=ctx_tiles, final=final),
        grid=(nsteps,),
        in_specs=[pl.BlockSpec((2, tile), lambda i: (0, i), memory_space=pltpu.SMEM),
                  pl.BlockSpec((2, tile), lambda i: (0, jnp.minimum(i + 1, nsteps - 1)), memory_space=pltpu.SMEM),
                  pl.BlockSpec(memory_space=pl.ANY),
                  pl.BlockSpec((tile, D_MODEL), lambda i: (i, 0)),
                  pl.BlockSpec((tile, LANES), lambda i: (i, 0)),
                  pl.BlockSpec((None, 1, D_MODEL), lambda i: (mrow(i), 0, 5)),
                  pl.BlockSpec((1, D_MODEL), lambda i: (0, 0))],
        out_specs=out_specs,
        out_shape=out_shape,
        scratch_shapes=[pltpu.VMEM((2, 2, tile, D_MODEL), F32), pltpu.SemaphoreType.DMA((2,))],
        compiler_params=_params(("arbitrary",), disable_bounds_checks=True),
        name="moe_combine",
    )(dest, dest, o, x1, gcol, mod3, fg)


def _axial_tables(n, dim):
    t = jnp.arange(n)
    rows = (t // GRID_W).astype(F32)
    cols = (t % GRID_W).astype(F32)
    nf = dim // 4
    inv = jnp.power(ROPE_BASE, -jnp.arange(nf, dtype=F32) / nf)
    ang = jnp.concatenate([rows[:, None] * inv, cols[:, None] * inv], axis=-1)
    return jnp.cos(ang), jnp.sin(ang)


def _rope_lane_tables(n, dim):
    cos, sin = _axial_tables(n, dim)
    c = jnp.concatenate([cos, cos], axis=-1)
    s = jnp.concatenate([-sin, sin], axis=-1)
    pad = LANES - dim
    if pad:
        c = jnp.pad(c, ((0, 0), (0, pad)))
        s = jnp.pad(s, ((0, 0), (0, pad)))
    return c, s


def _identity_rows(n, dim):
    c = jnp.pad(jnp.ones((n, dim), F32), ((0, 0), (0, LANES - dim)))
    return c, jnp.zeros((n, LANES), F32)


def _swap_halves_cols(w):
    half = w.shape[-1] // 2
    return jnp.concatenate([w[..., half:], w[..., :half]], axis=-1)


def _moe_plan(eid, n_slots):
    tm = MOE_TILE
    n = eid.shape[1]
    e = eid.reshape(-1)
    onehot = (e[:, None] == jnp.arange(N_EXPERTS, dtype=I32)[None, :]).astype(I32)
    csum = jnp.cumsum(onehot, axis=0)
    rank = jnp.sum(csum * onehot, axis=1) - 1
    counts = csum[-1]
    padded = (counts + tm - 1) // tm * tm
    pend = jnp.cumsum(padded)
    pstart = pend - padded
    dest = (jnp.sum(onehot * pstart[None, :], axis=1) + rank).reshape(2, n).astype(I32)
    n_valid = (pend[-1] // tm).astype(I32)
    blk_start = jnp.arange(n_slots // tm, dtype=I32) * tm
    block_e = jnp.minimum(jnp.sum((blk_start[:, None] >= pend[None, :]).astype(I32), axis=1), N_EXPERTS - 1)
    meta = jnp.stack([pstart + counts, padded - counts]).astype(I32)
    return dest, block_e.astype(I32), n_valid.reshape(1), meta


def kernel(x_prompt, x_sample, c, cache_ckv, cache_krope, state_ret, c_ctx, w_ada, b_ada, norm_attn, norm_ffn,
           w_in, ret_decay_logit, q_norm, kv_norm, w_uq, w_ukv, w_out, w_router, b_router, w_exp_gate, w_exp_up,
           w_exp_down, final_norm):
    batch, seq, d = x_prompt.shape
    dbatch, dseq, _ = x_sample.shape
    depth = w_ada.shape[0]
    past = cache_ckv.shape[2]
    n_ctx, n_lat = batch * seq, dbatch * dseq
    n = n_ctx + n_lat
    assert d == D_MODEL and dbatch + 1 <= 8
    assert n_ctx % dseq == 0 and dseq % ROW_TILE == 0 and n_ctx % ROW_TILE == 0 and seq % RET_CHUNK == 0
    n_slots = (2 * n // MOE_TILE + N_EXPERTS) * MOE_TILE

    x = jnp.concatenate([x_prompt.reshape(n_ctx, d), x_sample.reshape(n_lat, d)], axis=0)
    cond = jnp.concatenate([c_ctx[None, :], c, jnp.zeros((8 - 1 - dbatch, d), F32)], axis=0)
    mod = _ada_call(cond, w_ada, b_ada)

    cr, sr = _rope_lane_tables(dseq, D_RET)
    cm, sm = _rope_lane_tables(dseq, D_ROPE)
    one_q, zero_q = _identity_rows(ROW_TILE, D_ROPE)
    cos_q, sin_q = jnp.concatenate([one_q, cm], axis=0), jnp.concatenate([zero_q, sm], axis=0)
    one_p, zero_p = _identity_rows(past, D_ROPE)
    cos_k, sin_k = jnp.concatenate([one_p, cm], axis=0), jnp.concatenate([zero_p, sm], axis=0)
    one_c, zero_c = _identity_rows(seq, D_ROPE)
    eye = jnp.eye(LANES, dtype=BF16)

    wr = jnp.pad(w_router, ((0, 0), (0, LANES - N_EXPERTS))).astype(BF16)
    br = b_router.reshape(N_EXPERTS, 1).astype(F32)
    lg_all = jax.nn.log_sigmoid(ret_decay_logit.astype(F32))

    ckv_layers, krope_layers, state_layers = [], [], []
    for l in range(depth):
        mod3 = mod[l].reshape(8, 1, 6 * d)
        wl = w_in[l]
        cuts = 4 * RET_W + Q_RANK + KV_RANK
        w_in_p = jnp.concatenate(
            [wl, _swap_halves_cols(wl[:, cuts:]), jnp.zeros((d, IN_W - wl.shape[1] - D_ROPE), F32)],
            axis=1).astype(BF16)
        wq = w_uq[l].reshape(Q_RANK, H_MLA, D_NOPE + D_ROPE)
        w_uq_p = jnp.concatenate([wq, _swap_halves_cols(wq[..., D_NOPE:])], axis=-1)
        w_uq_p = w_uq_p.reshape(Q_RANK, H_MLA * Q_HEAD_W).astype(BF16)
        wkv = w_ukv[l].reshape(KV_RANK, H_MLA, D_NOPE + D_V)
        wkt = wkv[..., :D_NOPE].transpose(1, 2, 0).astype(BF16)
        wv = wkv[..., D_NOPE:].transpose(1, 0, 2).astype(BF16)
        wo = w_out[l].astype(BF16)
        wg, wu, wd = w_exp_gate[l].astype(BF16), w_exp_up[l].astype(BF16), w_exp_down[l].astype(BF16)

        qkvg, cqn, ckv, kr = _inproj_call(x, mod3, norm_attn[l][None, :], w_in_p, q_norm[l][None, :],
                                          kv_norm[l][None, :], n_ctx, dseq)
        ckv_layers.append(ckv[:n_ctx].reshape(batch, seq, KV_RANK))
        krope_layers.append(kr[:n_ctx, :D_ROPE].reshape(batch, seq, D_ROPE))

        ret_c, s_ctx = _retention_call(lg_all[l], qkvg, batch, seq, 0, H_RET, seq // RET_CHUNK, emit_state=True)
        (ret_l,) = _retention_call(lg_all[l], qkvg, dbatch, dseq, n_ctx // dseq, 2, 2, s0=state_ret[:, l],
                                   tables=(cr, sr))
        state_layers.append(s_ctx)

        q = _qproj_call(cqn, w_uq_p, cos_q, sin_q, n_ctx, dseq)
        kt_c, v_c = _kvproj_call(ckv[:n_ctx].reshape(batch, seq, KV_RANK).astype(BF16),
                                 kr[:n_ctx].reshape(batch, seq, LANES), one_c, zero_c, wkt, wv, eye, H_MLA)
        keys_ckv = jnp.concatenate([cache_ckv[:, l], ckv[n_ctx:].reshape(dbatch, dseq, KV_RANK)], axis=1).astype(BF16)
        keys_kr = jnp.concatenate([jnp.pad(cache_krope[:, l], ((0, 0), (0, 0), (0, LANES - D_ROPE))),
                                   kr[n_ctx:].reshape(dbatch, dseq, LANES)], axis=1)
        kt_l, v_l = _kvproj_call(keys_ckv, keys_kr, cos_k, sin_k, wkt, wv, eye, 2)
        mla_c = _attn_call(q, kt_c, v_c, seq, 0, seq, H_MLA)
        mla_l = _attn_call(q, kt_l, v_l, dseq, n_ctx, Q_TILE, 4)

        x1, hp, eid, gcol = _outproj_call(x, (ret_c, ret_l), (mla_c, mla_l), wo, mod3, norm_ffn[l][None, :],
                                          wr, br, n_ctx, dseq)

        dest, block_e, n_valid, meta = _moe_plan(eid, n_slots)
        xs = _dispatch_call(meta, dest, hp, n_slots)
        o = _expert_call(block_e, n_valid, xs, wg, wu, wd)
        x = _combine_call(dest, o, x1, gcol, mod3, final_norm[None, :], n_ctx, dseq, l == depth - 1)

    y_ctx, y_lat = x
    return (y_ctx.reshape(batch, seq, d), y_lat.reshape(dbatch, dseq, d), jnp.stack(ckv_layers, axis=1),
            jnp.stack(krope_layers, axis=1), jnp.stack(state_layers, axis=1))
```

```python
import functools
import math

import jax
import jax.numpy as jnp
from jax import lax
from jax.experimental import pallas as pl
from jax.experimental.pallas import tpu as pltpu

F32 = jnp.float32
BF16 = jnp.bfloat16
I32 = jnp.int32

D_MODEL = 2048
GRID_W = 64
H_RET = 8
D_RET = 128
RET_W = H_RET * D_RET
RET_CHUNK = 128
H_MLA = 8
D_NOPE = 128
D_ROPE = 64
D_V = 128
Q_RANK = 512
KV_RANK = 256
N_EXPERTS = 16
N_GROUPS = 4
EXPERTS_PER_GROUP = N_EXPERTS // N_GROUPS
D_EXPERT = 1024
ROPE_BASE = 10000.0
EPS = 1e-6

LANES = 128
Q_HEAD_W = 2 * LANES
LAT_W = 1024
IN_W = 4 * RET_W + LAT_W
ROW_TILE = 512
OUT_TILE = 256
Q_TILE = 256
KEY_CHUNK = 512
MOE_TILE = 256
GATHER_TILE = 256
DMA_UNROLL = 8
VMEM_LIMIT = 56 * 1024 * 1024


def _params(sem, **kw):
    return pltpu.CompilerParams(dimension_semantics=sem, vmem_limit_bytes=VMEM_LIMIT, **kw)


def _silu(x):
    return x * jax.nn.sigmoid(x)


def _mod_row(i, tile, n_ctx, seq_lat):
    r0 = i * tile
    return jnp.where(r0 < n_ctx, 0, 1 + (r0 - n_ctx) // seq_lat)


def _pos_block(i, tile, n_ctx, seq_lat):
    r0 = i * tile
    return jnp.where(r0 < n_ctx, 0, 1 + ((r0 - n_ctx) % seq_lat) // tile)


def _ada_kernel(c_ref, w_ref, b_ref, o_ref):
    a = _silu(c_ref[...]).astype(BF16)
    o_ref[...] = jnp.dot(a, w_ref[...].astype(BF16), preferred_element_type=F32) + b_ref[...]


def _ada_call(cond, w_ada, b_ada):
    depth, d, n6 = w_ada.shape
    tn = 1536
    return pl.pallas_call(
        _ada_kernel,
        grid=(depth, n6 // tn),
        in_specs=[pl.BlockSpec((8, d), lambda l, j: (0, 0)),
                  pl.BlockSpec((None, d, tn), lambda l, j: (l, 0, j)),
                  pl.BlockSpec((None, 1, tn), lambda l, j: (l, 0, j))],
        out_specs=pl.BlockSpec((None, 8, tn), lambda l, j: (l, 0, j)),
        out_shape=jax.ShapeDtypeStruct((depth, 8, n6), F32),
        compiler_params=_params(("arbitrary", "arbitrary")),
        name="adaln",
    )(cond, w_ada, b_ada.reshape(depth, 1, n6))


def _inproj_kernel(x_ref, sh_ref, sc_ref, ng_ref, w_ref, qg_ref, kvg_ref, qkvg_ref, cq_ref, ckv_ref, kr_ref):
    tm = x_ref.shape[0]
    for r0 in range(0, tm, tm // 2):
        rs = slice(r0, r0 + tm // 2)
        _project_rows(x_ref[rs, :], sh_ref, sc_ref, ng_ref, w_ref, qg_ref, kvg_ref, qkvg_ref, cq_ref, ckv_ref,
                      kr_ref, rs)


def _inproj_call(x, mod3, ng, w_in_p, qg, kvg, n_ctx, seq_lat):
    n = x.shape[0]
    tm = ROW_TILE
    mrow = functools.partial(_mod_row, tile=tm, n_ctx=n_ctx, seq_lat=seq_lat)
    return pl.pallas_call(
        _inproj_kernel,
        grid=(n // tm,),
        in_specs=[pl.BlockSpec((tm, D_MODEL), lambda i: (i, 0)),
                  pl.BlockSpec((None, 1, D_MODEL), lambda i: (mrow(i), 0, 0)),
                  pl.BlockSpec((None, 1, D_MODEL), lambda i: (mrow(i), 0, 1)),
                  pl.BlockSpec((1, D_MODEL), lambda i: (0, 0)),
                  pl.BlockSpec((D_MODEL, IN_W), lambda i: (0, 0), pipeline_mode=pl.Buffered(1)),
                  pl.BlockSpec((1, Q_RANK), lambda i: (0, 0)),
                  pl.BlockSpec((1, KV_RANK), lambda i: (0, 0))],
        out_specs=[pl.BlockSpec((tm, 4 * RET_W), lambda i: (i, 0)),
                   pl.BlockSpec((tm, Q_RANK), lambda i: (i, 0)),
                   pl.BlockSpec((tm, KV_RANK), lambda i: (i, 0)),
                   pl.BlockSpec((tm, LANES), lambda i: (i, 0))],
        out_shape=[jax.ShapeDtypeStruct((n, 4 * RET_W), BF16),
                   jax.ShapeDtypeStruct((n, Q_RANK), BF16),
                   jax.ShapeDtypeStruct((n, KV_RANK), F32),
                   jax.ShapeDtypeStruct((n, LANES), F32)],
        compiler_params=_params(("arbitrary",)),
        name="inproj",
    )(x, mod3, mod3, ng, w_in_p, qg, kvg)


def _project_rows(x, sh_ref, sc_ref, ng_ref, w_ref, qg_ref, kvg_ref, qkvg_ref, cq_ref, ckv_ref, kr_ref, rs,
                  after_chunk=lambda j: None):
    y = x * lax.rsqrt(jnp.mean(x * x, axis=-1, keepdims=True) + EPS) * ng_ref[...]
    h = (y * (1.0 + sc_ref[...]) + sh_ref[...]).astype(BF16)
    for j in range(4 * RET_W // LAT_W):
        cs = slice(j * LAT_W, (j + 1) * LAT_W)
        qkvg_ref[rs, cs] = jnp.dot(h, w_ref[:, cs], preferred_element_type=F32).astype(BF16)
        after_chunk(j)
    acc = jnp.dot(h, w_ref[:, 4 * RET_W:], preferred_element_type=F32)
    cq = acc[:, :Q_RANK]
    cq = cq * lax.rsqrt(jnp.mean(cq * cq, axis=-1, keepdims=True) + EPS) * qg_ref[...]
    cq_ref[rs, :] = cq.astype(BF16)
    ckv = acc[:, Q_RANK:Q_RANK + KV_RANK]
    ckv_ref[rs, :] = ckv * lax.rsqrt(jnp.mean(ckv * ckv, axis=-1, keepdims=True) + EPS) * kvg_ref[...]
    kr_ref[rs, :] = acc[:, Q_RANK + KV_RANK:Q_RANK + KV_RANK + LANES]


def _combine_inproj_kernel(dcur_ref, dnext_ref, o_ref, x1_ref, gcol_ref, g2_ref, sh_ref, sc_ref, ng_ref, w_ref,
                           qg_ref, kvg_ref, x2_ref, qkvg_ref, cq_ref, ckv_ref, kr_ref, buf, sem, *, tile, nsteps):
    i = pl.program_id(0)
    slot = i % 2

    def row_copy(dref, s, k, t):
        return pltpu.make_async_copy(o_ref.at[pl.ds(dref[k, t], 1)], buf.at[s, k, pl.ds(t, 1)], sem.at[s])

    @pl.when(i == 0)
    def _():
        def body(t, carry):
            for k in range(2):
                row_copy(dcur_ref, 0, k, t).start()
            return carry
        lax.fori_loop(0, tile, body, 0, unroll=DMA_UNROLL)

    for k in range(2):
        pltpu.make_async_copy(o_ref.at[pl.ds(0, tile)], buf.at[slot, k], sem.at[slot]).wait()

    gc = gcol_ref[...]
    x2 = x1_ref[...] + g2_ref[...] * (gc[:, 0:1] * buf[slot, 0] + gc[:, 1:2] * buf[slot, 1])
    x2_ref[...] = x2

    nchunks = 4 * RET_W // LAT_W
    share = tile // nchunks

    def request_rows(j):
        for t in range(j * share, (j + 1) * share):
            for k in range(2):
                row_copy(dnext_ref, 1 - slot, k, t).start()

    _project_rows(x2, sh_ref, sc_ref, ng_ref, w_ref, qg_ref, kvg_ref, qkvg_ref, cq_ref, ckv_ref, kr_ref,
                  slice(None), after_chunk=request_rows)

    @pl.when(i == nsteps - 1)
    def _():
        for k in range(2):
            pltpu.make_async_copy(o_ref.at[pl.ds(0, tile)], buf.at[1 - slot, k], sem.at[1 - slot]).wait()


def _combine_inproj_call(dest, o, x1, gcol, mod_prev, mod3, ng, w_in_p, qg, kvg, n_ctx, seq_lat):
    n = x1.shape[0]
    tile = GATHER_TILE
    nsteps = n // tile
    mrow = functools.partial(_mod_row, tile=tile, n_ctx=n_ctx, seq_lat=seq_lat)

    def rows(width):
        return pl.BlockSpec((tile, width), lambda i: (i, 0))

    def const(shape):
        return pl.BlockSpec(shape, lambda i: (0, 0))

    return pl.pallas_call(
        functools.partial(_combine_inproj_kernel, tile=tile, nsteps=nsteps),
        grid=(nsteps,),
        in_specs=[pl.BlockSpec((2, tile), lambda i: (0, i), memory_space=pltpu.SMEM),
                  pl.BlockSpec((2, tile), lambda i: (0, jnp.minimum(i + 1, nsteps - 1)), memory_space=pltpu.SMEM),
                  pl.BlockSpec(memory_space=pl.ANY),
                  rows(D_MODEL), rows(LANES),
                  pl.BlockSpec((None, 1, D_MODEL), lambda i: (mrow(i), 0, 5)),
                  pl.BlockSpec((None, 1, D_MODEL), lambda i: (mrow(i), 0, 0)),
                  pl.BlockSpec((None, 1, D_MODEL), lambda i: (mrow(i), 0, 1)),
                  const((1, D_MODEL)),
                  pl.BlockSpec((D_MODEL, IN_W), lambda i: (0, 0), pipeline_mode=pl.Buffered(1)),
                  const((1, Q_RANK)), const((1, KV_RANK))],
        out_specs=[rows(D_MODEL), rows(4 * RET_W), rows(Q_RANK), rows(KV_RANK), rows(LANES)],
        out_shape=[jax.ShapeDtypeStruct((n, D_MODEL), F32),
                   jax.ShapeDtypeStruct((n, 4 * RET_W), BF16),
                   jax.ShapeDtypeStruct((n, Q_RANK), BF16),
                   jax.ShapeDtypeStruct((n, KV_RANK), F32),
                   jax.ShapeDtypeStruct((n, LANES), F32)],
        scratch_shapes=[pltpu.VMEM((2, 2, tile, D_MODEL), F32), pltpu.SemaphoreType.DMA((2,))],
        compiler_params=_params(("arbitrary",), disable_bounds_checks=True),
        name="combine_inproj",
    )(dest, dest, o, x1, gcol, mod_prev, mod3, mod3, ng, w_in_p, qg, kvg)


def _retention_kernel(lg_ref, *refs, n, hb, unroll, rope, has_s0, emit_state):
    it = iter(refs)
    q_ref, k_ref, v_ref, g_ref = next(it), next(it), next(it), next(it)
    s0_ref = next(it) if has_s0 else None
    cos_ref, sin_ref = (next(it), next(it)) if rope else (None, None)
    o_ref = next(it)
    sfin_ref = next(it) if emit_state else None
    kvf_scr, kvb_scr, kt_scr, dm_scr = next(it), next(it), next(it), next(it)

    C = RET_CHUNK
    nc = n // C
    hblk = pl.program_id(1)
    diff = (lax.broadcasted_iota(I32, (C, C), 0) - lax.broadcasted_iota(I32, (C, C), 1)).astype(F32)
    col = lax.broadcasted_iota(I32, (C, 1), 0).astype(F32)
    lane = lax.broadcasted_iota(I32, (1, C), 1).astype(F32)
    k_scale = D_RET ** -0.5

    xi_f, xi_b, zeta_f, zeta_b, g_f, g_b = [], [], [], [], [], []
    for hh in range(hb):
        lgf = lg_ref[0, hblk * hb + hh]
        lgb = lg_ref[1, hblk * hb + hh]
        dm_scr[hh] = (jnp.where(diff >= 0, jnp.exp(jnp.maximum(diff, 0.0) * lgf), 0.0)
                      + jnp.where(diff <= 0, jnp.exp(jnp.maximum(-diff, 0.0) * lgb), 0.0))
        xi_f.append(jnp.exp((col + 1.0) * lgf))
        xi_b.append(jnp.exp((C - col) * lgb))
        zeta_f.append(jnp.exp((C - 1.0 - lane) * lgf))
        zeta_b.append(jnp.exp(lane * lgb))
        g_f.append(jnp.exp(jnp.full((1, C), float(C), F32) * lgf))
        g_b.append(jnp.exp(jnp.full((1, C), float(C), F32) * lgb))

    def rows(c):
        return pl.ds(pl.multiple_of(c * C, C), C)

    def cols(hh):
        return slice(hh * C, (hh + 1) * C)

    def rot(x, c):
        if not rope:
            return x
        return x * cos_ref[rows(c), :] + pltpu.roll(x, C // 2, 1) * sin_ref[rows(c), :]

    def phase_a(c, carry):
        for hh in range(hb):
            kt = rot(k_ref[rows(c), cols(hh)].astype(F32) * k_scale, c).T
            kt_scr[hh, c] = kt.astype(BF16)
            vc = v_ref[rows(c), cols(hh)]
            kvf_scr[hh, c] = jnp.dot((kt * zeta_f[hh]).astype(BF16), vc, preferred_element_type=F32)
            kvb_scr[hh, c] = jnp.dot((kt * zeta_b[hh]).astype(BF16), vc, preferred_element_type=F32)
        return carry

    lax.fori_loop(0, nc, phase_a, 0, unroll=unroll)

    for hh in range(hb):
        def scan(i, carry, hh=hh):
            sf, sb = carry
            cb = nc - 1 - i
            upd_f = kvf_scr[hh, i]
            kvf_scr[hh, i] = sf
            upd_b = kvb_scr[hh, cb]
            kvb_scr[hh, cb] = sb
            return g_f[hh] * sf + upd_f, g_b[hh] * sb + upd_b

        if has_s0:
            init = (s0_ref[0, hh], s0_ref[1, hh])
        else:
            init = (jnp.zeros((C, C), F32), jnp.zeros((C, C), F32))
        s_f, s_b = lax.fori_loop(0, nc, scan, init)
        if emit_state:
            sfin_ref[0, hh] = s_f
            sfin_ref[1, hh] = s_b

    def phase_c(c, carry):
        for hh in range(hb):
            qc = rot(q_ref[rows(c), cols(hh)].astype(F32), c)
            inner = jnp.dot(qc.astype(BF16), kt_scr[hh, c], preferred_element_type=F32) * dm_scr[hh]
            o = (jnp.dot(inner.astype(BF16), v_ref[rows(c), cols(hh)], preferred_element_type=F32)
                 + jnp.dot((qc * xi_f[hh]).astype(BF16), kvf_scr[hh, c].astype(BF16), preferred_element_type=F32)
                 + jnp.dot((qc * xi_b[hh]).astype(BF16), kvb_scr[hh, c].astype(BF16), preferred_element_type=F32))
            dev = o - jnp.mean(o, axis=-1, keepdims=True)
            on = dev * lax.rsqrt(jnp.mean(dev * dev, axis=-1, keepdims=True) + EPS)
            gate = g_ref[rows(c), cols(hh)].astype(F32)
            o_ref[rows(c), cols(hh)] = (on * _silu(gate)).astype(BF16)
        return carry

    lax.fori_loop(0, nc, phase_c, 0, unroll=unroll)


def _retention_call(lg2, qkvg, batch, n, row_blk0, hb, unroll, *, s0=None, tables=None, emit_state=False):
    hblks = H_RET // hb
    w = hb * D_RET
    nc = n // RET_CHUNK

    def qspec(part):
        return pl.BlockSpec((n, w), lambda b, h, lg: (row_blk0 + b, part * hblks + h))

    in_specs = [qspec(0), qspec(1), qspec(2), qspec(3)]
    args = [qkvg, qkvg, qkvg, qkvg]
    if s0 is not None:
        in_specs.append(pl.BlockSpec((None, 2, hb, D_RET, D_RET), lambda b, h, lg: (b, 0, h, 0, 0)))
        args.append(s0)
    if tables is not None:
        in_specs += [pl.BlockSpec((n, D_RET), lambda b, h, lg: (0, 0))] * 2
        args += list(tables)
    out_specs = [pl.BlockSpec((n, w), lambda b, h, lg: (b, h))]
    out_shape = [jax.ShapeDtypeStruct((batch * n, RET_W), BF16)]
    if emit_state:
        out_specs.append(pl.BlockSpec((None, 2, hb, D_RET, D_RET), lambda b, h, lg: (b, 0, h, 0, 0)))
        out_shape.append(jax.ShapeDtypeStruct((batch, 2, H_RET, D_RET, D_RET), F32))
    kern = functools.partial(_retention_kernel, n=n, hb=hb, unroll=unroll, rope=tables is not None,
                             has_s0=s0 is not None, emit_state=emit_state)
    return pl.pallas_call(
        kern,
        grid_spec=pltpu.PrefetchScalarGridSpec(
            num_scalar_prefetch=1, grid=(batch, hblks), in_specs=in_specs, out_specs=out_specs,
            scratch_shapes=[pltpu.VMEM((hb, nc, D_RET, D_RET), F32), pltpu.VMEM((hb, nc, D_RET, D_RET), F32),
                            pltpu.VMEM((hb, nc, D_RET, D_RET), BF16), pltpu.VMEM((hb, D_RET, D_RET), F32)]),
        out_shape=out_shape,
        compiler_params=_params(("arbitrary", "arbitrary")),
        name="retention",
    )(lg2, *args)


def _qproj_kernel(cq_ref, w_ref, cos_ref, sin_ref, q_ref):
    scale = (D_NOPE + D_ROPE) ** -0.5 * math.log2(math.e)
    acc = jnp.dot(cq_ref[...], w_ref[...], preferred_element_type=F32)
    for h in range(H_MLA):
        base = h * Q_HEAD_W
        q_ref[:, base:base + LANES] = (acc[:, base:base + LANES] * scale).astype(BF16)
        tail = acc[:, base + LANES:base + Q_HEAD_W]
        rot = tail * cos_ref[...] + pltpu.roll(tail, D_ROPE, 1) * sin_ref[...]
        q_ref[:, base + LANES:base + Q_HEAD_W] = (rot * scale).astype(BF16)


def _qproj_call(cqn, w_uq_p, cos_q, sin_q, n_ctx, seq_lat):
    n = cqn.shape[0]
    tm = ROW_TILE
    pblk = functools.partial(_pos_block, tile=tm, n_ctx=n_ctx, seq_lat=seq_lat)
    return pl.pallas_call(
        _qproj_kernel,
        grid=(n // tm,),
        in_specs=[pl.BlockSpec((tm, Q_RANK), lambda i: (i, 0)),
                  pl.BlockSpec((Q_RANK, H_MLA * Q_HEAD_W), lambda i: (0, 0)),
                  pl.BlockSpec((tm, LANES), lambda i: (pblk(i), 0)),
                  pl.BlockSpec((tm, LANES), lambda i: (pblk(i), 0))],
        out_specs=pl.BlockSpec((tm, H_MLA * Q_HEAD_W), lambda i: (i, 0)),
        out_shape=jax.ShapeDtypeStruct((n, H_MLA * Q_HEAD_W), BF16),
        compiler_params=_params(("arbitrary",)),
        name="qproj",
    )(cqn, w_uq_p, cos_q, sin_q)


def _kvproj_kernel(ckv_ref, kr_ref, cos_ref, sin_ref, wkt_ref, wv_ref, eye_ref, kt_ref, v_ref, *, hb):
    nt = (((1,), (1,)), ((), ()))
    keys = ckv_ref[...]
    kr = kr_ref[...]
    rot = (kr * cos_ref[...] + pltpu.roll(kr, D_ROPE, 1) * sin_ref[...]).astype(BF16)
    rope_t = lax.dot_general(eye_ref[...], rot, nt, preferred_element_type=F32).astype(BF16)
    for hh in range(hb):
        kt_ref[hh, :D_NOPE, :] = lax.dot_general(wkt_ref[hh], keys, nt, preferred_element_type=F32).astype(BF16)
        kt_ref[hh, D_NOPE:, :] = rope_t
        v_ref[hh] = jnp.dot(keys, wv_ref[hh], preferred_element_type=F32).astype(BF16)


def _kvproj_call(keys_ckv, keys_kr, cos_k, sin_k, wkt, wv, eye, hb):
    batch, nk, _ = keys_ckv.shape
    hblks = H_MLA // hb
    return pl.pallas_call(
        functools.partial(_kvproj_kernel, hb=hb),
        grid=(batch, hblks),
        in_specs=[pl.BlockSpec((None, nk, KV_RANK), lambda b, h: (b, 0, 0)),
                  pl.BlockSpec((None, nk, LANES), lambda b, h: (b, 0, 0)),
                  pl.BlockSpec((nk, LANES), lambda b, h: (0, 0)),
                  pl.BlockSpec((nk, LANES), lambda b, h: (0, 0)),
                  pl.BlockSpec((hb, D_NOPE, KV_RANK), lambda b, h: (h, 0, 0)),
                  pl.BlockSpec((hb, KV_RANK, D_V), lambda b, h: (h, 0, 0)),
                  pl.BlockSpec((LANES, LANES), lambda b, h: (0, 0))],
        out_specs=[pl.BlockSpec((None, hb, Q_HEAD_W, nk), lambda b, h: (b, h, 0, 0)),
                   pl.BlockSpec((None, hb, nk, D_V), lambda b, h: (b, h, 0, 0))],
        out_shape=[jax.ShapeDtypeStruct((batch, H_MLA, Q_HEAD_W, nk), BF16),
                   jax.ShapeDtypeStruct((batch, H_MLA, nk, D_V), BF16)],
        compiler_params=_params(("arbitrary", "arbitrary")),
        name="kvproj",
    )(keys_ckv, keys_kr, cos_k, sin_k, wkt, wv, eye)


def _attn_kernel(q_ref, kt_ref, v_ref, o_ref, *, hb):
    nk = kt_ref.shape[-1]
    kc = KEY_CHUNK if nk % KEY_CHUNK == 0 else nk
    for hh in range(hb):
        q = q_ref[:, hh * Q_HEAD_W:(hh + 1) * Q_HEAD_W]
        s = jnp.dot(q, kt_ref[hh], preferred_element_type=F32)
        m = jnp.max(s, axis=-1, keepdims=True)
        den = acc = None
        for j in range(nk // kc):
            p = jnp.exp2(s[:, j * kc:(j + 1) * kc] - m)
            psum = p[:, :LANES]
            for g in range(1, kc // LANES):
                psum = psum + p[:, g * LANES:(g + 1) * LANES]
            pv = jnp.dot(p.astype(BF16), v_ref[hh, j * kc:(j + 1) * kc, :], preferred_element_type=F32)
            den, acc = (psum, pv) if j == 0 else (den + psum, acc + pv)
        o = acc / jnp.sum(den, axis=-1, keepdims=True)
        o_ref[:, hh * D_V:(hh + 1) * D_V] = o.astype(BF16)


def _attn_call(q, kt, v, n, row0, tq, hb):
    batch, _, _, nk = kt.shape
    hblks = H_MLA // hb
    nq = n // tq
    blk0 = row0 // tq
    return pl.pallas_call(
        functools.partial(_attn_kernel, hb=hb),
        grid=(batch, hblks, nq),
        in_specs=[pl.BlockSpec((tq, hb * Q_HEAD_W), lambda b, h, i: (blk0 + b * nq + i, h)),
                  pl.BlockSpec((None, hb, Q_HEAD_W, nk), lambda b, h, i: (b, h, 0, 0)),
                  pl.BlockSpec((None, hb, nk, D_V), lambda b, h, i: (b, h, 0, 0))],
        out_specs=pl.BlockSpec((tq, hb * D_V), lambda b, h, i: (b * nq + i, h)),
        out_shape=jax.ShapeDtypeStruct((batch * n, H_MLA * D_V), BF16),
        compiler_params=_params(("arbitrary", "arbitrary", "arbitrary")),
        name="mla_attention",
    )(q, kt, v)


def _route_rows(s_rows, sel_rows):
    epg = EXPERTS_PER_GROUP
    best_g = best_v = None
    for g in range(N_GROUPS):
        v = sel_rows[g * epg:(g + 1) * epg]
        top2 = None
        for a in range(epg):
            for b in range(a + 1, epg):
                pair = v[a] + v[b]
                top2 = pair if top2 is None else jnp.maximum(top2, pair)
        if g == 0:
            best_v, best_g = top2, jnp.zeros(top2.shape, I32)
        else:
            upd = top2 > best_v
            best_v = jnp.where(upd, top2, best_v)
            best_g = jnp.where(upd, g, best_g)

    def pick(rows_, i):
        out = rows_[i]
        for g in range(1, N_GROUPS):
            out = jnp.where(best_g == g, rows_[g * epg + i], out)
        return out

    w = [pick(sel_rows, i) for i in range(epg)]
    sv = [pick(s_rows, i) for i in range(epg)]
    m0, i0, s0 = w[0], jnp.zeros(w[0].shape, I32), sv[0]
    for i in range(1, epg):
        upd = w[i] > m0
        m0, i0, s0 = jnp.where(upd, w[i], m0), jnp.where(upd, i, i0), jnp.where(upd, sv[i], s0)
    m1 = jnp.full(w[0].shape, -jnp.inf, F32)
    i1, s1 = jnp.zeros(w[0].shape, I32), sv[0]
    for i in range(epg):
        upd = (i0 != i) & (w[i] > m1)
        m1, i1, s1 = jnp.where(upd, w[i], m1), jnp.where(upd, i, i1), jnp.where(upd, sv[i], s1)
    tot = s0 + s1
    return best_g * epg + i0, best_g * epg + i1, s0 / tot, s1 / tot


def _outproj_kernel(x_ref, mrc_ref, mrl_ref, mmc_ref, mml_ref, wo_ref, g1_ref, sc_ref, sh_ref, ng_ref, wr_ref,
                    br_ref, x1_ref, hp_ref, eid_ref, rank_ref, gcol_ref, cnt_ref, *, ctx_tiles):
    half = D_MODEL // 2

    @pl.when(pl.program_id(0) == 0)
    def _():
        cnt_ref[...] = jnp.zeros(cnt_ref.shape, F32)

    is_ctx = pl.program_id(0) < ctx_tiles
    mr = jnp.where(is_ctx, mrc_ref[...], mrl_ref[...])
    mm = jnp.where(is_ctx, mmc_ref[...], mml_ref[...])
    mix = (jnp.dot(mr, wo_ref[:RET_W, :], preferred_element_type=F32)
           + jnp.dot(mm, wo_ref[RET_W:, :], preferred_element_type=F32))
    x1 = x_ref[...] + g1_ref[...] * mix
    x1_ref[...] = x1
    y = x1 * lax.rsqrt(jnp.mean(x1 * x1, axis=-1, keepdims=True) + EPS) * ng_ref[...]
    h2 = (y * (1.0 + sc_ref[...]) + sh_ref[...]).astype(BF16)
    hi = lax.bitcast_convert_type(h2[:, :half].astype(F32), I32)
    lo = lax.bitcast_convert_type(h2[:, half:].astype(F32), I32)
    hp_ref[...] = (hi & jnp.int32(-65536)) | lax.shift_right_logical(lo, 16)

    logits = jnp.dot(h2, wr_ref[...], preferred_element_type=F32)
    lt = logits.T
    s_rows = [jax.nn.sigmoid(lt[e:e + 1, :]) for e in range(N_EXPERTS)]
    sel_rows = [s_rows[e] + br_ref[e:e + 1, :] for e in range(N_EXPERTS)]
    e0, e1, g0, g1 = _route_rows(s_rows, sel_rows)
    eid_ref[0:1, :] = e0
    eid_ref[1:2, :] = e1
    t = g0.shape[1]
    erow = lax.broadcasted_iota(I32, (N_EXPERTS, t), 0)
    hit0 = erow == e0
    hit1 = erow == e1
    onehot = jnp.where(hit0 | hit1, 1.0, 0.0)
    before = lax.broadcasted_iota(I32, (t, t), 0) < lax.broadcasted_iota(I32, (t, t), 1)
    prefix = jnp.dot(onehot.astype(BF16), jnp.where(before, 1.0, 0.0).astype(BF16), preferred_element_type=F32)
    seen = prefix + cnt_ref[:, 0:1]
    rank_ref[0:1, :] = jnp.sum(jnp.where(hit0, seen, 0.0), axis=0, keepdims=True).astype(I32)
    rank_ref[1:2, :] = jnp.sum(jnp.where(hit1, seen, 0.0), axis=0, keepdims=True).astype(I32)
    cnt_ref[...] = cnt_ref[...] + jnp.sum(onehot, axis=1, keepdims=True)
    row = lax.broadcasted_iota(I32, (LANES, t), 0)
    gcol_ref[...] = jnp.where(row == 0, g0, jnp.where(row == 1, g1, 0.0)).T


def _outproj_call(x, mix_ret, mix_mla, wo, mod3, ng, wr, br, n_ctx, seq_lat):
    n = x.shape[0]
    tm = OUT_TILE
    ctx_tiles = n_ctx // tm
    mrow = functools.partial(_mod_row, tile=tm, n_ctx=n_ctx, seq_lat=seq_lat)

    def mspec(chunk):
        return pl.BlockSpec((None, 1, D_MODEL), lambda i: (mrow(i), 0, chunk))

    def ctx_spec(width):
        return pl.BlockSpec((tm, width), lambda i: (jnp.minimum(i, ctx_tiles - 1), 0))

    def lat_spec(width):
        return pl.BlockSpec((tm, width), lambda i: (jnp.maximum(i - ctx_tiles, 0), 0))

    return pl.pallas_call(
        functools.partial(_outproj_kernel, ctx_tiles=ctx_tiles),
        grid=(n // tm,),
        in_specs=[pl.BlockSpec((tm, D_MODEL), lambda i: (i, 0)),
                  ctx_spec(RET_W), lat_spec(RET_W), ctx_spec(H_MLA * D_V), lat_spec(H_MLA * D_V),
                  pl.BlockSpec((D_MODEL, D_MODEL), lambda i: (0, 0)),
                  mspec(2), mspec(4), mspec(3),
                  pl.BlockSpec((1, D_MODEL), lambda i: (0, 0)),
                  pl.BlockSpec((D_MODEL, LANES), lambda i: (0, 0)),
                  pl.BlockSpec((N_EXPERTS, 1), lambda i: (0, 0))],
        out_specs=[pl.BlockSpec((tm, D_MODEL), lambda i: (i, 0)),
                   pl.BlockSpec((tm, D_MODEL // 2), lambda i: (i, 0)),
                   pl.BlockSpec((2, tm), lambda i: (0, i)),
                   pl.BlockSpec((2, tm), lambda i: (0, i)),
                   pl.BlockSpec((tm, LANES), lambda i: (i, 0)),
                   pl.BlockSpec((N_EXPERTS, LANES), lambda i: (0, 0))],
        out_shape=[jax.ShapeDtypeStruct((n, D_MODEL), F32),
                   jax.ShapeDtypeStruct((n, D_MODEL // 2), I32),
                   jax.ShapeDtypeStruct((2, n), I32),
                   jax.ShapeDtypeStruct((2, n), I32),
                   jax.ShapeDtypeStruct((n, LANES), F32),
                   jax.ShapeDtypeStruct((N_EXPERTS, LANES), F32)],
        compiler_params=_params(("arbitrary",)),
        name="outproj_router",
    )(x, mix_ret[0], mix_ret[1], mix_mla[0], mix_mla[1], wo, mod3, mod3, mod3, ng, wr, br)


def _dispatch_kernel(meta_ref, dest_ref, src_ref, out_ref, sem, *, tile, nsteps, nblocks):
    def row_to(t, slot, s):
        return pltpu.make_async_copy(src_ref.at[pl.ds(t, 1)], out_ref.at[pl.ds(slot, 1)], s)

    def issue(t, carry):
        row_to(t, dest_ref[0, t], sem.at[0]).start()
        row_to(t, dest_ref[1, t], sem.at[0]).start()
        return carry

    lax.fori_loop(0, tile, issue, 0, unroll=DMA_UNROLL)
    for _ in range(2):
        pltpu.make_async_copy(src_ref, out_ref.at[pl.ds(0, tile)], sem.at[0]).wait()

    @pl.when(pl.program_id(0) == nsteps - 1)
    def _():
        for e in range(N_EXPERTS):
            first = meta_ref[0, e]
            npad = meta_ref[1, e]

            def fill(r, carry):
                row_to(0, first + r, sem.at[1]).start()
                return carry

            lax.fori_loop(0, npad, fill, 0)

            def fdrain(r, carry):
                row_to(0, 0, sem.at[1]).wait()
                return carry

            lax.fori_loop(0, npad, fdrain, 0)

        def tail_copy(b):
            return pltpu.make_async_copy(src_ref, out_ref.at[pl.ds(pl.multiple_of(b * tile, tile), tile)], sem.at[1])

        def tail(b, carry):
            tail_copy(b).start()
            tail_copy(b).wait()
            return carry

        lax.fori_loop(meta_ref[2, 0], nblocks, tail, 0)


def _dispatch_call(meta, dest, hp, n_slots):
    n, wcols = hp.shape
    tile = GATHER_TILE
    nsteps = n // tile
    return pl.pallas_call(
        functools.partial(_dispatch_kernel, tile=tile, nsteps=nsteps, nblocks=n_slots // tile),
        grid_spec=pltpu.PrefetchScalarGridSpec(
            num_scalar_prefetch=1, grid=(nsteps,),
            in_specs=[pl.BlockSpec((2, tile), lambda i, m: (0, i), memory_space=pltpu.SMEM),
                      pl.BlockSpec((tile, wcols), lambda i, m: (i, 0))],
            out_specs=pl.BlockSpec(memory_space=pl.ANY),
            scratch_shapes=[pltpu.SemaphoreType.DMA((2,))]),
        out_shape=jax.ShapeDtypeStruct((n_slots, wcols), I32),
        compiler_params=_params(("arbitrary",), disable_bounds_checks=True),
        name="moe_dispatch",
    )(meta, dest, hp)


def _expert_kernel(be_ref, nv_ref, x_ref, wg_ref, wu_ref, wd_ref, o_ref):
    half = D_MODEL // 2

    @pl.when(pl.program_id(0) < nv_ref[0])
    def _():
        w = x_ref[...]
        xa = lax.bitcast_convert_type(w & jnp.int32(-65536), F32).astype(BF16)
        xb = lax.bitcast_convert_type(lax.shift_left(w, 16), F32).astype(BF16)
        hg = (jnp.dot(xa, wg_ref[:half, :], preferred_element_type=F32)
              + jnp.dot(xb, wg_ref[half:, :], preferred_element_type=F32))
        hu = (jnp.dot(xa, wu_ref[:half, :], preferred_element_type=F32)
              + jnp.dot(xb, wu_ref[half:, :], preferred_element_type=F32))
        act = (_silu(hg) * hu).astype(BF16)
        o_ref[...] = jnp.dot(act, wd_ref[...], preferred_element_type=F32)

    @pl.when(pl.program_id(0) >= nv_ref[0])
    def _():
        o_ref[...] = jnp.zeros(o_ref.shape, F32)


def _expert_call(block_e, n_valid, xs, wg, wu, wd, layer):
    n_slots = xs.shape[0]
    tm = MOE_TILE
    nb = n_slots // tm

    def blk(b, be, nv):
        return jnp.minimum(b, nv[0] - 1)

    def wmap(b, be, nv):
        return (layer, be[blk(b, be, nv)], 0, 0)

    return pl.pallas_call(
        _expert_kernel,
        grid_spec=pltpu.PrefetchScalarGridSpec(
            num_scalar_prefetch=2, grid=(nb,),
            in_specs=[pl.BlockSpec((tm, D_MODEL // 2), lambda b, be, nv: (blk(b, be, nv), 0)),
                      pl.BlockSpec((None, None, D_MODEL, D_EXPERT), wmap),
                      pl.BlockSpec((None, None, D_MODEL, D_EXPERT), wmap),
                      pl.BlockSpec((None, None, D_EXPERT, D_MODEL), wmap)],
            out_specs=pl.BlockSpec((tm, D_MODEL), lambda b, be, nv: (b, 0))),
        out_shape=jax.ShapeDtypeStruct((n_slots, D_MODEL), F32),
        compiler_params=_params(("arbitrary",)),
        name="moe_experts",
    )(block_e, n_valid, xs, wg, wu, wd)


def _combine_kernel(dcur_ref, dnext_ref, o_ref, x_ref, gcol_ref, g2_ref, fg_ref, yc_ref, yl_ref, buf, sem,
                    *, tile, nsteps, ctx_tiles):
    i = pl.program_id(0)
    slot = i % 2

    def issue(dref, s):
        def body(t, carry):
            for k in range(2):
                pltpu.make_async_copy(o_ref.at[pl.ds(dref[k, t], 1)], buf.at[s, k, pl.ds(t, 1)], sem.at[s]).start()
            return carry
        lax.fori_loop(0, tile, body, 0, unroll=DMA_UNROLL)

    @pl.when(i == 0)
    def _():
        issue(dcur_ref, 0)

    @pl.when(i + 1 < nsteps)
    def _():
        issue(dnext_ref, 1 - slot)

    for k in range(2):
        pltpu.make_async_copy(o_ref.at[pl.ds(0, tile)], buf.at[slot, k], sem.at[slot]).wait()

    gc = gcol_ref[...]
    ffn = gc[:, 0:1] * buf[slot, 0] + gc[:, 1:2] * buf[slot, 1]
    x2 = x_ref[...] + g2_ref[...] * ffn
    y = x2 * lax.rsqrt(jnp.mean(x2 * x2, axis=-1, keepdims=True) + EPS) * fg_ref[...]

    @pl.when(i < ctx_tiles)
    def _():
        yc_ref[...] = y

    @pl.when(i >= ctx_tiles)
    def _():
        yl_ref[...] = y


def _combine_call(dest, o, x1, gcol, mod3, fg, n_ctx, seq_lat):
    n = x1.shape[0]
    tile = GATHER_TILE
    nsteps = n // tile
    ctx_tiles = n_ctx // tile
    mrow = functools.partial(_mod_row, tile=tile, n_ctx=n_ctx, seq_lat=seq_lat)
    out_specs = [pl.BlockSpec((tile, D_MODEL), lambda i: (jnp.minimum(i, ctx_tiles - 1), 0)),
                 pl.BlockSpec((tile, D_MODEL), lambda i: (jnp.maximum(i - ctx_tiles, 0), 0))]
    out_shape = [jax.ShapeDtypeStruct((n_ctx, D_MODEL), F32), jax.ShapeDtypeStruct((n - n_ctx, D_MODEL), F32)]
    return pl.pallas_call(
        functools.partial(_combine_kernel, tile=tile, nsteps=nsteps, ctx_tiles=ctx_tiles),
        grid=(nsteps,),
        in_specs=[pl.BlockSpec((2, tile), lambda i: (0, i), memory_space=pltpu.SMEM),
                  pl.BlockSpec((2, tile), lambda i: (0, jnp.minimum(i + 1, nsteps - 1)), memory_space=pltpu.SMEM),
                  pl.BlockSpec(memory_space=pl.ANY),
                  pl.BlockSpec((tile, D_MODEL), lambda i: (i, 0)),
                  pl.BlockSpec((tile, LANES), lambda i: (i, 0)),
                  pl.BlockSpec((None, 1, D_MODEL), lambda i: (mrow(i), 0, 5)),
                  pl.BlockSpec((1, D_MODEL), lambda i: (0, 0))],
        out_specs=out_specs,
        out_shape=out_shape,
        scratch_shapes=[pltpu.VMEM((2, 2, tile, D_MODEL), F32), pltpu.SemaphoreType.DMA((2,))],
        compiler_params=_params(("arbitrary",), disable_bounds_checks=True),
        name="moe_combine",
    )(dest, dest, o, x1, gcol, mod3, fg)


def _axial_tables(n, dim):
    t = jnp.arange(n)
    rows = (t // GRID_W).astype(F32)
    cols = (t % GRID_W).astype(F32)
    nf = dim // 4
    inv = jnp.power(ROPE_BASE, -jnp.arange(nf, dtype=F32) / nf)
    ang = jnp.concatenate([rows[:, None] * inv, cols[:, None] * inv], axis=-1)
    return jnp.cos(ang), jnp.sin(ang)


def _rope_lane_tables(n, dim):
    cos, sin = _axial_tables(n, dim)
    c = jnp.concatenate([cos, cos], axis=-1)
    s = jnp.concatenate([-sin, sin], axis=-1)
    pad = LANES - dim
    if pad:
        c = jnp.pad(c, ((0, 0), (0, pad)))
        s = jnp.pad(s, ((0, 0), (0, pad)))
    return c, s


def _identity_rows(n, dim):
    c = jnp.pad(jnp.ones((n, dim), F32), ((0, 0), (0, LANES - dim)))
    return c, jnp.zeros((n, LANES), F32)


def _swap_halves_cols(w):
    half = w.shape[-1] // 2
    return jnp.concatenate([w[..., half:], w[..., :half]], axis=-1)


def _moe_plan(eid, rank, counts, n_slots):
    tm = MOE_TILE
    counts = counts.astype(I32)
    padded = (counts + tm - 1) // tm * tm
    pend = jnp.cumsum(padded)
    pstart = pend - padded
    dest = (pstart[eid] + rank).astype(I32)
    n_valid = (pend[-1] // tm).astype(I32)
    blk_start = jnp.arange(n_slots // tm, dtype=I32) * tm
    block_e = jnp.minimum(jnp.sum((blk_start[:, None] >= pend[None, :]).astype(I32), axis=1), N_EXPERTS - 1)
    meta = jnp.stack([pstart + counts, padded - counts, jnp.broadcast_to(n_valid, counts.shape)]).astype(I32)
    return dest, block_e.astype(I32), n_valid.reshape(1), meta


def kernel(x_prompt, x_sample, c, cache_ckv, cache_krope, state_ret, c_ctx, w_ada, b_ada, norm_attn, norm_ffn,
           w_in, ret_decay_logit, q_norm, kv_norm, w_uq, w_ukv, w_out, w_router, b_router, w_exp_gate, w_exp_up,
           w_exp_down, final_norm):
    batch, seq, d = x_prompt.shape
    dbatch, dseq, _ = x_sample.shape
    depth = w_ada.shape[0]
    past = cache_ckv.shape[2]
    n_ctx, n_lat = batch * seq, dbatch * dseq
    n = n_ctx + n_lat
    assert d == D_MODEL and dbatch + 1 <= 8
    assert n_ctx % dseq == 0 and dseq % ROW_TILE == 0 and n_ctx % ROW_TILE == 0 and seq % RET_CHUNK == 0
    n_slots = (2 * n // MOE_TILE + N_EXPERTS) * MOE_TILE

    x = jnp.concatenate([x_prompt.reshape(n_ctx, d), x_sample.reshape(n_lat, d)], axis=0)
    cond = jnp.concatenate([c_ctx[None, :], c, jnp.zeros((8 - 1 - dbatch, d), F32)], axis=0)
    mod = _ada_call(cond, w_ada, b_ada)

    cr, sr = _rope_lane_tables(dseq, D_RET)
    cm, sm = _rope_lane_tables(dseq, D_ROPE)
    one_q, zero_q = _identity_rows(ROW_TILE, D_ROPE)
    cos_q, sin_q = jnp.concatenate([one_q, cm], axis=0), jnp.concatenate([zero_q, sm], axis=0)
    one_p, zero_p = _identity_rows(past, D_ROPE)
    cos_k, sin_k = jnp.concatenate([one_p, cm], axis=0), jnp.concatenate([zero_p, sm], axis=0)
    one_c, zero_c = _identity_rows(seq, D_ROPE)
    eye = jnp.eye(LANES, dtype=BF16)

    wr = jnp.pad(w_router, ((0, 0), (0, LANES - N_EXPERTS))).astype(BF16)
    br = b_router.reshape(N_EXPERTS, 1).astype(F32)
    lg_all = jax.nn.log_sigmoid(ret_decay_logit.astype(F32))
    wg, wu, wd = w_exp_gate.astype(BF16), w_exp_up.astype(BF16), w_exp_down.astype(BF16)

    ckv_layers, krope_layers, state_layers = [], [], []
    pending = None
    for l in range(depth):
        mod3 = mod[l].reshape(8, 1, 6 * d)
        wl = w_in[l]
        cuts = 4 * RET_W + Q_RANK + KV_RANK
        w_in_p = jnp.concatenate(
            [wl, _swap_halves_cols(wl[:, cuts:]), jnp.zeros((d, IN_W - wl.shape[1] - D_ROPE), F32)],
            axis=1).astype(BF16)
        wq = w_uq[l].reshape(Q_RANK, H_MLA, D_NOPE + D_ROPE)
        w_uq_p = jnp.concatenate([wq, _swap_halves_cols(wq[..., D_NOPE:])], axis=-1)
        w_uq_p = w_uq_p.reshape(Q_RANK, H_MLA * Q_HEAD_W).astype(BF16)
        wkv = w_ukv[l].reshape(KV_RANK, H_MLA, D_NOPE + D_V)
        wkt = wkv[..., :D_NOPE].transpose(1, 2, 0).astype(BF16)
        wv = wkv[..., D_NOPE:].transpose(1, 0, 2).astype(BF16)
        wo = w_out[l].astype(BF16)

        if pending is None:
            qkvg, cqn, ckv, kr = _inproj_call(x, mod3, norm_attn[l][None, :], w_in_p, q_norm[l][None, :],
                                              kv_norm[l][None, :], n_ctx, dseq)
        else:
            x, qkvg, cqn, ckv, kr = _combine_inproj_call(*pending, mod3, norm_attn[l][None, :], w_in_p,
                                                         q_norm[l][None, :], kv_norm[l][None, :], n_ctx, dseq)
        ckv_layers.append(ckv[:n_ctx].reshape(batch, seq, KV_RANK))
        krope_layers.append(kr[:n_ctx, :D_ROPE].reshape(batch, seq, D_ROPE))

        ret_c, s_ctx = _retention_call(lg_all[l], qkvg, batch, seq, 0, H_RET, seq // RET_CHUNK, emit_state=True)
        (ret_l,) = _retention_call(lg_all[l], qkvg, dbatch, dseq, n_ctx // dseq, 2, 4, s0=state_ret[:, l],
                                   tables=(cr, sr))
        state_layers.append(s_ctx)

        q = _qproj_call(cqn, w_uq_p, cos_q, sin_q, n_ctx, dseq)
        kt_c, v_c = _kvproj_call(ckv[:n_ctx].reshape(batch, seq, KV_RANK).astype(BF16),
                                 kr[:n_ctx].reshape(batch, seq, LANES), one_c, zero_c, wkt, wv, eye, H_MLA)
        keys_ckv = jnp.concatenate([cache_ckv[:, l], ckv[n_ctx:].reshape(dbatch, dseq, KV_RANK)], axis=1).astype(BF16)
        keys_kr = jnp.concatenate([jnp.pad(cache_krope[:, l], ((0, 0), (0, 0), (0, LANES - D_ROPE))),
                                   kr[n_ctx:].reshape(dbatch, dseq, LANES)], axis=1)
        kt_l, v_l = _kvproj_call(keys_ckv, keys_kr, cos_k, sin_k, wkt, wv, eye, 2)
        mla_c = _attn_call(q, kt_c, v_c, seq, 0, seq, H_MLA)
        mla_l = _attn_call(q, kt_l, v_l, dseq, n_ctx, Q_TILE, 4)

        x1, hp, eid, rank, gcol, cnt = _outproj_call(x, (ret_c, ret_l), (mla_c, mla_l), wo, mod3,
                                                     norm_ffn[l][None, :], wr, br, n_ctx, dseq)

        dest, block_e, n_valid, meta = _moe_plan(eid, rank, cnt[:, 0], n_slots)
        xs = _dispatch_call(meta, dest, hp, n_slots)
        o = _expert_call(block_e, n_valid, xs, wg, wu, wd, l)
        pending = (dest, o, x1, gcol, mod3)

    y_ctx, y_lat = _combine_call(*pending, final_norm[None, :], n_ctx, dseq)
    return (y_ctx.reshape(batch, seq, d), y_lat.reshape(dbatch, dseq, d), jnp.stack(ckv_layers, axis=1),
            jnp.stack(krope_layers, axis=1), jnp.stack(state_layers, axis=1))
```

```python
import functools
import math

import jax
import jax.numpy as jnp
from jax import lax
from jax.experimental import pallas as pl
from jax.experimental.pallas import tpu as pltpu

F32 = jnp.float32
BF16 = jnp.bfloat16
I32 = jnp.int32

D_MODEL = 2048
GRID_W = 64
H_RET = 8
D_RET = 128
RET_W = H_RET * D_RET
RET_CHUNK = 128
H_MLA = 8
D_NOPE = 128
D_ROPE = 64
D_V = 128
Q_RANK = 512
KV_RANK = 256
N_EXPERTS = 16
N_GROUPS = 4
EXPERTS_PER_GROUP = N_EXPERTS // N_GROUPS
D_EXPERT = 1024
ROPE_BASE = 10000.0
EPS = 1e-6

LANES = 128
Q_HEAD_W = 2 * LANES
LAT_W = 1024
IN_W = 4 * RET_W + LAT_W
ROW_TILE = 512
OUT_TILE = 256
Q_TILE = 256
MOE_TILE = 256
GATHER_TILE = 256
DMA_UNROLL = 8
VMEM_LIMIT = 56 * 1024 * 1024


def _params(sem, **kw):
    return pltpu.CompilerParams(dimension_semantics=sem, vmem_limit_bytes=VMEM_LIMIT, **kw)


def _silu(x):
    return x * jax.nn.sigmoid(x)


def _mod_row(i, tile, n_ctx, seq_lat):
    r0 = i * tile
    return jnp.where(r0 < n_ctx, 0, 1 + (r0 - n_ctx) // seq_lat)


def _pos_block(i, tile, n_ctx, seq_lat):
    r0 = i * tile
    return jnp.where(r0 < n_ctx, 0, 1 + ((r0 - n_ctx) % seq_lat) // tile)


def _ada_kernel(c_ref, w_ref, b_ref, o_ref):
    a = _silu(c_ref[...]).astype(BF16)
    o_ref[...] = jnp.dot(a, w_ref[...].astype(BF16), preferred_element_type=F32) + b_ref[...]


def _ada_call(cond, w_ada, b_ada):
    depth, d, n6 = w_ada.shape
    tn = 1536
    return pl.pallas_call(
        _ada_kernel,
        grid=(depth, n6 // tn),
        in_specs=[pl.BlockSpec((8, d), lambda l, j: (0, 0)),
                  pl.BlockSpec((None, d, tn), lambda l, j: (l, 0, j)),
                  pl.BlockSpec((None, 1, tn), lambda l, j: (l, 0, j))],
        out_specs=pl.BlockSpec((None, 8, tn), lambda l, j: (l, 0, j)),
        out_shape=jax.ShapeDtypeStruct((depth, 8, n6), F32),
        compiler_params=_params(("arbitrary", "arbitrary")),
        name="adaln",
    )(cond, w_ada, b_ada.reshape(depth, 1, n6))


def _inproj_kernel(xc_ref, xl_ref, sh_ref, sc_ref, ng_ref, w_ref, qg_ref, kvg_ref, x_ref, qkvg_ref, cq_ref,
                   ckv_ref, kr_ref, *, ctx_tiles):
    x = jnp.where(pl.program_id(0) < ctx_tiles, xc_ref[...], xl_ref[...])
    x_ref[...] = x
    _project_rows(x, sh_ref, sc_ref, ng_ref, w_ref, qg_ref, kvg_ref, qkvg_ref, cq_ref, ckv_ref, kr_ref,
                  slice(None))


def _inproj_call(x_ctx, x_lat, mod3, ng, w_in_p, qg, kvg, seq_lat):
    n_ctx = x_ctx.shape[0]
    n = n_ctx + x_lat.shape[0]
    tm = GATHER_TILE
    ctx_tiles = n_ctx // tm
    mrow = functools.partial(_mod_row, tile=tm, n_ctx=n_ctx, seq_lat=seq_lat)
    return pl.pallas_call(
        functools.partial(_inproj_kernel, ctx_tiles=ctx_tiles),
        grid=(n // tm,),
        in_specs=[pl.BlockSpec((tm, D_MODEL), lambda i: (jnp.minimum(i, ctx_tiles - 1), 0)),
                  pl.BlockSpec((tm, D_MODEL), lambda i: (jnp.maximum(i - ctx_tiles, 0), 0)),
                  pl.BlockSpec((None, 1, D_MODEL), lambda i: (mrow(i), 0, 0)),
                  pl.BlockSpec((None, 1, D_MODEL), lambda i: (mrow(i), 0, 1)),
                  pl.BlockSpec((1, D_MODEL), lambda i: (0, 0)),
                  pl.BlockSpec((D_MODEL, IN_W), lambda i: (0, 0), pipeline_mode=pl.Buffered(1)),
                  pl.BlockSpec((1, Q_RANK), lambda i: (0, 0)),
                  pl.BlockSpec((1, KV_RANK), lambda i: (0, 0))],
        out_specs=[pl.BlockSpec((tm, D_MODEL), lambda i: (i, 0)),
                   pl.BlockSpec((tm, 4 * RET_W), lambda i: (i, 0)),
                   pl.BlockSpec((tm, Q_RANK), lambda i: (i, 0)),
                   pl.BlockSpec((tm, KV_RANK), lambda i: (i, 0)),
                   pl.BlockSpec((tm, LANES), lambda i: (i, 0))],
        out_shape=[jax.ShapeDtypeStruct((n, D_MODEL), F32),
                   jax.ShapeDtypeStruct((n, 4 * RET_W), BF16),
                   jax.ShapeDtypeStruct((n, Q_RANK), BF16),
                   jax.ShapeDtypeStruct((n, KV_RANK), F32),
                   jax.ShapeDtypeStruct((n, LANES), F32)],
        compiler_params=_params(("arbitrary",)),
        name="inproj",
    )(x_ctx, x_lat, mod3, mod3, ng, w_in_p, qg, kvg)


def _project_rows(x, sh_ref, sc_ref, ng_ref, w_ref, qg_ref, kvg_ref, qkvg_ref, cq_ref, ckv_ref, kr_ref, rs,
                  after_chunk=lambda j: None):
    y = x * lax.rsqrt(jnp.mean(x * x, axis=-1, keepdims=True) + EPS) * ng_ref[...]
    h = (y * (1.0 + sc_ref[...]) + sh_ref[...]).astype(BF16)
    for j in range(4 * RET_W // LAT_W):
        cs = slice(j * LAT_W, (j + 1) * LAT_W)
        qkvg_ref[rs, cs] = jnp.dot(h, w_ref[:, cs], preferred_element_type=F32).astype(BF16)
        after_chunk(j)
    acc = jnp.dot(h, w_ref[:, 4 * RET_W:], preferred_element_type=F32)
    cq = acc[:, :Q_RANK]
    cq = cq * lax.rsqrt(jnp.mean(cq * cq, axis=-1, keepdims=True) + EPS) * qg_ref[...]
    cq_ref[rs, :] = cq.astype(BF16)
    ckv = acc[:, Q_RANK:Q_RANK + KV_RANK]
    ckv_ref[rs, :] = ckv * lax.rsqrt(jnp.mean(ckv * ckv, axis=-1, keepdims=True) + EPS) * kvg_ref[...]
    kr_ref[rs, :] = acc[:, Q_RANK + KV_RANK:Q_RANK + KV_RANK + LANES]


def _combine_inproj_kernel(dcur_ref, dnext_ref, o_ref, x1_ref, gcol_ref, g2_ref, sh_ref, sc_ref, ng_ref, w_ref,
                           qg_ref, kvg_ref, x2_ref, qkvg_ref, cq_ref, ckv_ref, kr_ref, buf, sem, *, tile, nsteps):
    i = pl.program_id(0)
    slot = i % 2

    def row_copy(dref, s, k, t):
        return pltpu.make_async_copy(o_ref.at[pl.ds(dref[k, t], 1)], buf.at[s, k, pl.ds(t, 1)], sem.at[s])

    @pl.when(i == 0)
    def _():
        def body(t, carry):
            for k in range(2):
                row_copy(dcur_ref, 0, k, t).start()
            return carry
        lax.fori_loop(0, tile, body, 0, unroll=DMA_UNROLL)

    for k in range(2):
        pltpu.make_async_copy(o_ref.at[pl.ds(0, tile)], buf.at[slot, k], sem.at[slot]).wait()

    gc = gcol_ref[...]
    x2 = x1_ref[...] + g2_ref[...] * (gc[:, 0:1] * buf[slot, 0] + gc[:, 1:2] * buf[slot, 1])
    x2_ref[...] = x2

    nchunks = 4 * RET_W // LAT_W
    share = tile // nchunks

    def request_rows(j):
        for t in range(j * share, (j + 1) * share):
            for k in range(2):
                row_copy(dnext_ref, 1 - slot, k, t).start()

    _project_rows(x2, sh_ref, sc_ref, ng_ref, w_ref, qg_ref, kvg_ref, qkvg_ref, cq_ref, ckv_ref, kr_ref,
                  slice(None), after_chunk=request_rows)

    @pl.when(i == nsteps - 1)
    def _():
        for k in range(2):
            pltpu.make_async_copy(o_ref.at[pl.ds(0, tile)], buf.at[1 - slot, k], sem.at[1 - slot]).wait()


def _combine_inproj_call(dest, o, x1, gcol, mod_prev, mod3, ng, w_in_p, qg, kvg, n_ctx, seq_lat):
    n = x1.shape[0]
    tile = GATHER_TILE
    nsteps = n // tile
    mrow = functools.partial(_mod_row, tile=tile, n_ctx=n_ctx, seq_lat=seq_lat)

    def rows(width):
        return pl.BlockSpec((tile, width), lambda i: (i, 0))

    def const(shape):
        return pl.BlockSpec(shape, lambda i: (0, 0))

    return pl.pallas_call(
        functools.partial(_combine_inproj_kernel, tile=tile, nsteps=nsteps),
        grid=(nsteps,),
        in_specs=[pl.BlockSpec((2, tile), lambda i: (0, i), memory_space=pltpu.SMEM),
                  pl.BlockSpec((2, tile), lambda i: (0, jnp.minimum(i + 1, nsteps - 1)), memory_space=pltpu.SMEM),
                  pl.BlockSpec(memory_space=pl.ANY),
                  rows(D_MODEL), rows(LANES),
                  pl.BlockSpec((None, 1, D_MODEL), lambda i: (mrow(i), 0, 5)),
                  pl.BlockSpec((None, 1, D_MODEL), lambda i: (mrow(i), 0, 0)),
                  pl.BlockSpec((None, 1, D_MODEL), lambda i: (mrow(i), 0, 1)),
                  const((1, D_MODEL)),
                  pl.BlockSpec((D_MODEL, IN_W), lambda i: (0, 0), pipeline_mode=pl.Buffered(1)),
                  const((1, Q_RANK)), const((1, KV_RANK))],
        out_specs=[rows(D_MODEL), rows(4 * RET_W), rows(Q_RANK), rows(KV_RANK), rows(LANES)],
        out_shape=[jax.ShapeDtypeStruct((n, D_MODEL), F32),
                   jax.ShapeDtypeStruct((n, 4 * RET_W), BF16),
                   jax.ShapeDtypeStruct((n, Q_RANK), BF16),
                   jax.ShapeDtypeStruct((n, KV_RANK), F32),
                   jax.ShapeDtypeStruct((n, LANES), F32)],
        scratch_shapes=[pltpu.VMEM((2, 2, tile, D_MODEL), F32), pltpu.SemaphoreType.DMA((2,))],
        compiler_params=_params(("arbitrary",), disable_bounds_checks=True),
        name="combine_inproj",
    )(dest, dest, o, x1, gcol, mod_prev, mod3, mod3, ng, w_in_p, qg, kvg)


def _retention_kernel(lg_ref, *refs, n, hb, unroll, rope, has_s0, emit_state):
    it = iter(refs)
    q_ref, k_ref, v_ref, g_ref = next(it), next(it), next(it), next(it)
    s0_ref = next(it) if has_s0 else None
    cos_ref, sin_ref = (next(it), next(it)) if rope else (None, None)
    o_ref = next(it)
    sfin_ref = next(it) if emit_state else None
    kvf_scr, kvb_scr, kt_scr, dm_scr = next(it), next(it), next(it), next(it)

    C = RET_CHUNK
    nc = n // C
    hblk = pl.program_id(1)
    diff = (lax.broadcasted_iota(I32, (C, C), 0) - lax.broadcasted_iota(I32, (C, C), 1)).astype(F32)
    col = lax.broadcasted_iota(I32, (C, 1), 0).astype(F32)
    lane = lax.broadcasted_iota(I32, (1, C), 1).astype(F32)
    k_scale = D_RET ** -0.5

    xi_f, xi_b, zeta_f, zeta_b, g_f, g_b = [], [], [], [], [], []
    for hh in range(hb):
        lgf = lg_ref[0, hblk * hb + hh]
        lgb = lg_ref[1, hblk * hb + hh]
        dm_scr[hh] = (jnp.where(diff >= 0, jnp.exp(jnp.maximum(diff, 0.0) * lgf), 0.0)
                      + jnp.where(diff <= 0, jnp.exp(jnp.maximum(-diff, 0.0) * lgb), 0.0))
        xi_f.append(jnp.exp((col + 1.0) * lgf))
        xi_b.append(jnp.exp((C - col) * lgb))
        zeta_f.append(jnp.exp((C - 1.0 - lane) * lgf))
        zeta_b.append(jnp.exp(lane * lgb))
        g_f.append(jnp.exp(jnp.full((1, C), float(C), F32) * lgf))
        g_b.append(jnp.exp(jnp.full((1, C), float(C), F32) * lgb))

    def rows(c):
        return pl.ds(pl.multiple_of(c * C, C), C)

    def cols(hh):
        return slice(hh * C, (hh + 1) * C)

    def rot(x, c):
        if not rope:
            return x
        return x * cos_ref[rows(c), :] + pltpu.roll(x, C // 2, 1) * sin_ref[rows(c), :]

    def phase_a(c, carry):
        for hh in range(hb):
            kt = rot(k_ref[rows(c), cols(hh)].astype(F32) * k_scale, c).T
            kt_scr[hh, c] = kt.astype(BF16)
            vc = v_ref[rows(c), cols(hh)]
            kvf_scr[hh, c] = jnp.dot((kt * zeta_f[hh]).astype(BF16), vc, preferred_element_type=F32)
            kvb_scr[hh, c] = jnp.dot((kt * zeta_b[hh]).astype(BF16), vc, preferred_element_type=F32)
        return carry

    lax.fori_loop(0, nc, phase_a, 0, unroll=unroll)

    for hh in range(hb):
        def scan(i, carry, hh=hh):
            sf, sb = carry
            cb = nc - 1 - i
            upd_f = kvf_scr[hh, i]
            kvf_scr[hh, i] = sf
            upd_b = kvb_scr[hh, cb]
            kvb_scr[hh, cb] = sb
            return g_f[hh] * sf + upd_f, g_b[hh] * sb + upd_b

        if has_s0:
            init = (s0_ref[0, hh], s0_ref[1, hh])
        else:
            init = (jnp.zeros((C, C), F32), jnp.zeros((C, C), F32))
        s_f, s_b = lax.fori_loop(0, nc, scan, init)
        if emit_state:
            sfin_ref[0, hh] = s_f
            sfin_ref[1, hh] = s_b

    def phase_c(c, carry):
        for hh in range(hb):
            qc = rot(q_ref[rows(c), cols(hh)].astype(F32), c)
            inner = jnp.dot(qc.astype(BF16), kt_scr[hh, c], preferred_element_type=F32) * dm_scr[hh]
            o = (jnp.dot(inner.astype(BF16), v_ref[rows(c), cols(hh)], preferred_element_type=F32)
                 + jnp.dot((qc * xi_f[hh]).astype(BF16), kvf_scr[hh, c].astype(BF16), preferred_element_type=F32)
                 + jnp.dot((qc * xi_b[hh]).astype(BF16), kvb_scr[hh, c].astype(BF16), preferred_element_type=F32))
            dev = o - jnp.mean(o, axis=-1, keepdims=True)
            on = dev * lax.rsqrt(jnp.mean(dev * dev, axis=-1, keepdims=True) + EPS)
            gate = g_ref[rows(c), cols(hh)].astype(F32)
            o_ref[rows(c), cols(hh)] = (on * _silu(gate)).astype(BF16)
        return carry

    lax.fori_loop(0, nc, phase_c, 0, unroll=unroll)


def _retention_call(lg2, qkvg, batch, n, row_blk0, hb, unroll, *, s0=None, tables=None, emit_state=False):
    hblks = H_RET // hb
    w = hb * D_RET
    nc = n // RET_CHUNK

    def qspec(part):
        return pl.BlockSpec((n, w), lambda b, h, lg: (row_blk0 + b, part * hblks + h))

    in_specs = [qspec(0), qspec(1), qspec(2), qspec(3)]
    args = [qkvg, qkvg, qkvg, qkvg]
    if s0 is not None:
        in_specs.append(pl.BlockSpec((None, 2, hb, D_RET, D_RET), lambda b, h, lg: (b, 0, h, 0, 0)))
        args.append(s0)
    if tables is not None:
        in_specs += [pl.BlockSpec((n, D_RET), lambda b, h, lg: (0, 0))] * 2
        args += list(tables)
    out_specs = [pl.BlockSpec((n, w), lambda b, h, lg: (b, h))]
    out_shape = [jax.ShapeDtypeStruct((batch * n, RET_W), BF16)]
    if emit_state:
        out_specs.append(pl.BlockSpec((None, 2, hb, D_RET, D_RET), lambda b, h, lg: (b, 0, h, 0, 0)))
        out_shape.append(jax.ShapeDtypeStruct((batch, 2, H_RET, D_RET, D_RET), F32))
    kern = functools.partial(_retention_kernel, n=n, hb=hb, unroll=unroll, rope=tables is not None,
                             has_s0=s0 is not None, emit_state=emit_state)
    return pl.pallas_call(
        kern,
        grid_spec=pltpu.PrefetchScalarGridSpec(
            num_scalar_prefetch=1, grid=(batch, hblks), in_specs=in_specs, out_specs=out_specs,
            scratch_shapes=[pltpu.VMEM((hb, nc, D_RET, D_RET), F32), pltpu.VMEM((hb, nc, D_RET, D_RET), F32),
                            pltpu.VMEM((hb, nc, D_RET, D_RET), BF16), pltpu.VMEM((hb, D_RET, D_RET), F32)]),
        out_shape=out_shape,
        compiler_params=_params(("arbitrary", "arbitrary")),
        name="retention",
    )(lg2, *args)


def _qproj_kernel(cq_ref, w_ref, cos_ref, sin_ref, q_ref):
    scale = (D_NOPE + D_ROPE) ** -0.5 * math.log2(math.e)
    acc = jnp.dot(cq_ref[...], w_ref[...], preferred_element_type=F32)
    for h in range(H_MLA):
        base = h * Q_HEAD_W
        q_ref[:, base:base + LANES] = (acc[:, base:base + LANES] * scale).astype(BF16)
        tail = acc[:, base + LANES:base + Q_HEAD_W]
        rot = tail * cos_ref[...] + pltpu.roll(tail, D_ROPE, 1) * sin_ref[...]
        q_ref[:, base + LANES:base + Q_HEAD_W] = (rot * scale).astype(BF16)


def _qproj_call(cqn, w_uq_p, cos_q, sin_q, n_ctx, seq_lat):
    n = cqn.shape[0]
    tm = ROW_TILE
    pblk = functools.partial(_pos_block, tile=tm, n_ctx=n_ctx, seq_lat=seq_lat)
    return pl.pallas_call(
        _qproj_kernel,
        grid=(n // tm,),
        in_specs=[pl.BlockSpec((tm, Q_RANK), lambda i: (i, 0)),
                  pl.BlockSpec((Q_RANK, H_MLA * Q_HEAD_W), lambda i: (0, 0)),
                  pl.BlockSpec((tm, LANES), lambda i: (pblk(i), 0)),
                  pl.BlockSpec((tm, LANES), lambda i: (pblk(i), 0))],
        out_specs=pl.BlockSpec((tm, H_MLA * Q_HEAD_W), lambda i: (i, 0)),
        out_shape=jax.ShapeDtypeStruct((n, H_MLA * Q_HEAD_W), BF16),
        compiler_params=_params(("arbitrary",)),
        name="qproj",
    )(cqn, w_uq_p, cos_q, sin_q)


def _kvproj_kernel(ckv_ref, kr_ref, cos_ref, sin_ref, wkt_ref, wv_ref, eye_ref, kt_ref, v_ref, *, hb):
    nt = (((1,), (1,)), ((), ()))
    keys = ckv_ref[...]
    kr = kr_ref[...]
    rot = (kr * cos_ref[...] + pltpu.roll(kr, D_ROPE, 1) * sin_ref[...]).astype(BF16)
    rope_t = lax.dot_general(eye_ref[...], rot, nt, preferred_element_type=F32).astype(BF16)
    for hh in range(hb):
        kt_ref[hh, :D_NOPE, :] = lax.dot_general(wkt_ref[hh], keys, nt, preferred_element_type=F32).astype(BF16)
        kt_ref[hh, D_NOPE:, :] = rope_t
        v_ref[hh] = jnp.dot(keys, wv_ref[hh], preferred_element_type=F32).astype(BF16)


def _kvproj_call(keys_ckv, keys_kr, cos_k, sin_k, wkt, wv, eye, hb):
    batch, nk, _ = keys_ckv.shape
    hblks = H_MLA // hb
    return pl.pallas_call(
        functools.partial(_kvproj_kernel, hb=hb),
        grid=(batch, hblks),
        in_specs=[pl.BlockSpec((None, nk, KV_RANK), lambda b, h: (b, 0, 0)),
                  pl.BlockSpec((None, nk, LANES), lambda b, h: (b, 0, 0)),
                  pl.BlockSpec((nk, LANES), lambda b, h: (0, 0)),
                  pl.BlockSpec((nk, LANES), lambda b, h: (0, 0)),
                  pl.BlockSpec((hb, D_NOPE, KV_RANK), lambda b, h: (h, 0, 0)),
                  pl.BlockSpec((hb, KV_RANK, D_V), lambda b, h: (h, 0, 0)),
                  pl.BlockSpec((LANES, LANES), lambda b, h: (0, 0))],
        out_specs=[pl.BlockSpec((None, hb, Q_HEAD_W, nk), lambda b, h: (b, h, 0, 0)),
                   pl.BlockSpec((None, hb, nk, D_V), lambda b, h: (b, h, 0, 0))],
        out_shape=[jax.ShapeDtypeStruct((batch, H_MLA, Q_HEAD_W, nk), BF16),
                   jax.ShapeDtypeStruct((batch, H_MLA, nk, D_V), BF16)],
        compiler_params=_params(("arbitrary", "arbitrary")),
        name="kvproj",
    )(keys_ckv, keys_kr, cos_k, sin_k, wkt, wv, eye)


def _attn_kernel(q_ref, kt_ref, v_ref, o_ref, *, hb):
    for hh in range(hb):
        q = q_ref[:, hh * Q_HEAD_W:(hh + 1) * Q_HEAD_W]
        s = jnp.dot(q, kt_ref[hh], preferred_element_type=F32)
        p = jnp.exp2(s - jnp.max(s, axis=-1, keepdims=True))
        den = jnp.sum(p, axis=-1, keepdims=True)
        o = jnp.dot(p.astype(BF16), v_ref[hh], preferred_element_type=F32)
        o_ref[:, hh * D_V:(hh + 1) * D_V] = (o / den).astype(BF16)


def _attn_call(q, kt, v, n, row0, tq, hb):
    batch, _, _, nk = kt.shape
    hblks = H_MLA // hb
    nq = n // tq
    blk0 = row0 // tq
    return pl.pallas_call(
        functools.partial(_attn_kernel, hb=hb),
        grid=(batch, hblks, nq),
        in_specs=[pl.BlockSpec((tq, hb * Q_HEAD_W), lambda b, h, i: (blk0 + b * nq + i, h)),
                  pl.BlockSpec((None, hb, Q_HEAD_W, nk), lambda b, h, i: (b, h, 0, 0)),
                  pl.BlockSpec((None, hb, nk, D_V), lambda b, h, i: (b, h, 0, 0))],
        out_specs=pl.BlockSpec((tq, hb * D_V), lambda b, h, i: (b * nq + i, h)),
        out_shape=jax.ShapeDtypeStruct((batch * n, H_MLA * D_V), BF16),
        compiler_params=_params(("arbitrary", "arbitrary", "arbitrary")),
        name="mla_attention",
    )(q, kt, v)


def _route_rows(s_rows, sel_rows):
    epg = EXPERTS_PER_GROUP
    best_g = best_v = None
    for g in range(N_GROUPS):
        v = sel_rows[g * epg:(g + 1) * epg]
        top2 = None
        for a in range(epg):
            for b in range(a + 1, epg):
                pair = v[a] + v[b]
                top2 = pair if top2 is None else jnp.maximum(top2, pair)
        if g == 0:
            best_v, best_g = top2, jnp.zeros(top2.shape, I32)
        else:
            upd = top2 > best_v
            best_v = jnp.where(upd, top2, best_v)
            best_g = jnp.where(upd, g, best_g)

    def pick(rows_, i):
        out = rows_[i]
        for g in range(1, N_GROUPS):
            out = jnp.where(best_g == g, rows_[g * epg + i], out)
        return out

    w = [pick(sel_rows, i) for i in range(epg)]
    sv = [pick(s_rows, i) for i in range(epg)]
    m0, i0, s0 = w[0], jnp.zeros(w[0].shape, I32), sv[0]
    for i in range(1, epg):
        upd = w[i] > m0
        m0, i0, s0 = jnp.where(upd, w[i], m0), jnp.where(upd, i, i0), jnp.where(upd, sv[i], s0)
    m1 = jnp.full(w[0].shape, -jnp.inf, F32)
    i1, s1 = jnp.zeros(w[0].shape, I32), sv[0]
    for i in range(epg):
        upd = (i0 != i) & (w[i] > m1)
        m1, i1, s1 = jnp.where(upd, w[i], m1), jnp.where(upd, i, i1), jnp.where(upd, sv[i], s1)
    tot = s0 + s1
    return best_g * epg + i0, best_g * epg + i1, s0 / tot, s1 / tot


def _outproj_kernel(x_ref, mrc_ref, mrl_ref, mmc_ref, mml_ref, wo_ref, g1_ref, sc_ref, sh_ref, ng_ref, wr_ref,
                    br_ref, x1_ref, hp_ref, eid_ref, rank_ref, gcol_ref, cnt_ref, *, ctx_tiles):
    half = D_MODEL // 2

    @pl.when(pl.program_id(0) == 0)
    def _():
        cnt_ref[...] = jnp.zeros(cnt_ref.shape, F32)

    is_ctx = pl.program_id(0) < ctx_tiles
    mr = jnp.where(is_ctx, mrc_ref[...], mrl_ref[...])
    mm = jnp.where(is_ctx, mmc_ref[...], mml_ref[...])
    mix = (jnp.dot(mr, wo_ref[:RET_W, :], preferred_element_type=F32)
           + jnp.dot(mm, wo_ref[RET_W:, :], preferred_element_type=F32))
    x1 = x_ref[...] + g1_ref[...] * mix
    x1_ref[...] = x1
    y = x1 * lax.rsqrt(jnp.mean(x1 * x1, axis=-1, keepdims=True) + EPS) * ng_ref[...]
    h2 = (y * (1.0 + sc_ref[...]) + sh_ref[...]).astype(BF16)
    hi = lax.bitcast_convert_type(h2[:, :half].astype(F32), I32)
    lo = lax.bitcast_convert_type(h2[:, half:].astype(F32), I32)
    hp_ref[...] = (hi & jnp.int32(-65536)) | lax.shift_right_logical(lo, 16)

    logits = jnp.dot(h2, wr_ref[...], preferred_element_type=F32)
    lt = logits.T
    s_rows = [jax.nn.sigmoid(lt[e:e + 1, :]) for e in range(N_EXPERTS)]
    sel_rows = [s_rows[e] + br_ref[e:e + 1, :] for e in range(N_EXPERTS)]
    e0, e1, g0, g1 = _route_rows(s_rows, sel_rows)
    eid_ref[0:1, :] = e0
    eid_ref[1:2, :] = e1
    t = g0.shape[1]
    erow = lax.broadcasted_iota(I32, (N_EXPERTS, t), 0)
    hit0 = erow == e0
    hit1 = erow == e1
    onehot = jnp.where(hit0 | hit1, 1.0, 0.0)
    before = lax.broadcasted_iota(I32, (t, t), 0) < lax.broadcasted_iota(I32, (t, t), 1)
    prefix = jnp.dot(onehot.astype(BF16), jnp.where(before, 1.0, 0.0).astype(BF16), preferred_element_type=F32)
    seen = prefix + cnt_ref[:, 0:1]
    rank_ref[0:1, :] = jnp.sum(jnp.where(hit0, seen, 0.0), axis=0, keepdims=True).astype(I32)
    rank_ref[1:2, :] = jnp.sum(jnp.where(hit1, seen, 0.0), axis=0, keepdims=True).astype(I32)
    cnt_ref[...] = cnt_ref[...] + jnp.sum(onehot, axis=1, keepdims=True)
    row = lax.broadcasted_iota(I32, (LANES, t), 0)
    gcol_ref[...] = jnp.where(row == 0, g0, jnp.where(row == 1, g1, 0.0)).T


def _outproj_call(x, mix_ret, mix_mla, wo, mod3, ng, wr, br, n_ctx, seq_lat):
    n = x.shape[0]
    tm = OUT_TILE
    ctx_tiles = n_ctx // tm
    mrow = functools.partial(_mod_row, tile=tm, n_ctx=n_ctx, seq_lat=seq_lat)

    def mspec(chunk):
        return pl.BlockSpec((None, 1, D_MODEL), lambda i: (mrow(i), 0, chunk))

    def ctx_spec(width):
        return pl.BlockSpec((tm, width), lambda i: (jnp.minimum(i, ctx_tiles - 1), 0))

    def lat_spec(width):
        return pl.BlockSpec((tm, width), lambda i: (jnp.maximum(i - ctx_tiles, 0), 0))

    return pl.pallas_call(
        functools.partial(_outproj_kernel, ctx_tiles=ctx_tiles),
        grid=(n // tm,),
        in_specs=[pl.BlockSpec((tm, D_MODEL), lambda i: (i, 0)),
                  ctx_spec(RET_W), lat_spec(RET_W), ctx_spec(H_MLA * D_V), lat_spec(H_MLA * D_V),
                  pl.BlockSpec((D_MODEL, D_MODEL), lambda i: (0, 0)),
                  mspec(2), mspec(4), mspec(3),
                  pl.BlockSpec((1, D_MODEL), lambda i: (0, 0)),
                  pl.BlockSpec((D_MODEL, LANES), lambda i: (0, 0)),
                  pl.BlockSpec((N_EXPERTS, 1), lambda i: (0, 0))],
        out_specs=[pl.BlockSpec((tm, D_MODEL), lambda i: (i, 0)),
                   pl.BlockSpec((tm, D_MODEL // 2), lambda i: (i, 0)),
                   pl.BlockSpec((2, tm), lambda i: (0, i)),
                   pl.BlockSpec((2, tm), lambda i: (0, i)),
                   pl.BlockSpec((tm, LANES), lambda i: (i, 0)),
                   pl.BlockSpec((N_EXPERTS, LANES), lambda i: (0, 0))],
        out_shape=[jax.ShapeDtypeStruct((n, D_MODEL), F32),
                   jax.ShapeDtypeStruct((n, D_MODEL // 2), I32),
                   jax.ShapeDtypeStruct((2, n), I32),
                   jax.ShapeDtypeStruct((2, n), I32),
                   jax.ShapeDtypeStruct((n, LANES), F32),
                   jax.ShapeDtypeStruct((N_EXPERTS, LANES), F32)],
        compiler_params=_params(("arbitrary",)),
        name="outproj_router",
    )(x, mix_ret[0], mix_ret[1], mix_mla[0], mix_mla[1], wo, mod3, mod3, mod3, ng, wr, br)


def _dispatch_kernel(meta_ref, dest_ref, src_ref, out_ref, sem, *, tile, nsteps, nblocks):
    def row_to(t, slot, s):
        return pltpu.make_async_copy(src_ref.at[pl.ds(t, 1)], out_ref.at[pl.ds(slot, 1)], s)

    def issue(t, carry):
        row_to(t, dest_ref[0, t], sem.at[0]).start()
        row_to(t, dest_ref[1, t], sem.at[0]).start()
        return carry

    lax.fori_loop(0, tile, issue, 0, unroll=DMA_UNROLL)
    for _ in range(2):
        pltpu.make_async_copy(src_ref, out_ref.at[pl.ds(0, tile)], sem.at[0]).wait()

    @pl.when(pl.program_id(0) == nsteps - 1)
    def _():
        for e in range(N_EXPERTS):
            first = meta_ref[0, e]
            npad = meta_ref[1, e]

            def fill(r, carry):
                row_to(0, first + r, sem.at[1]).start()
                return carry

            lax.fori_loop(0, npad, fill, 0)

            def fdrain(r, carry):
                row_to(0, 0, sem.at[1]).wait()
                return carry

            lax.fori_loop(0, npad, fdrain, 0)

        def tail_copy(b):
            return pltpu.make_async_copy(src_ref, out_ref.at[pl.ds(pl.multiple_of(b * tile, tile), tile)], sem.at[1])

        def tail(b, carry):
            tail_copy(b).start()
            tail_copy(b).wait()
            return carry

        lax.fori_loop(meta_ref[2, 0], nblocks, tail, 0)


def _dispatch_call(meta, dest, hp, n_slots):
    n, wcols = hp.shape
    tile = GATHER_TILE
    nsteps = n // tile
    return pl.pallas_call(
        functools.partial(_dispatch_kernel, tile=tile, nsteps=nsteps, nblocks=n_slots // tile),
        grid_spec=pltpu.PrefetchScalarGridSpec(
            num_scalar_prefetch=1, grid=(nsteps,),
            in_specs=[pl.BlockSpec((2, tile), lambda i, m: (0, i), memory_space=pltpu.SMEM),
                      pl.BlockSpec((tile, wcols), lambda i, m: (i, 0))],
            out_specs=pl.BlockSpec(memory_space=pl.ANY),
            scratch_shapes=[pltpu.SemaphoreType.DMA((2,))]),
        out_shape=jax.ShapeDtypeStruct((n_slots, wcols), I32),
        compiler_params=_params(("arbitrary",), disable_bounds_checks=True),
        name="moe_dispatch",
    )(meta, dest, hp)


def _expert_kernel(be_ref, nv_ref, x_ref, wg_ref, wu_ref, wd_ref, o_ref):
    half = D_MODEL // 2

    @pl.when(pl.program_id(0) < nv_ref[0])
    def _():
        w = x_ref[...]
        xa = lax.bitcast_convert_type(w & jnp.int32(-65536), F32).astype(BF16)
        xb = lax.bitcast_convert_type(lax.shift_left(w, 16), F32).astype(BF16)
        hg = (jnp.dot(xa, wg_ref[:half, :], preferred_element_type=F32)
              + jnp.dot(xb, wg_ref[half:, :], preferred_element_type=F32))
        hu = (jnp.dot(xa, wu_ref[:half, :], preferred_element_type=F32)
              + jnp.dot(xb, wu_ref[half:, :], preferred_element_type=F32))
        act = (_silu(hg) * hu).astype(BF16)
        o_ref[...] = jnp.dot(act, wd_ref[...], preferred_element_type=F32)

    @pl.when(pl.program_id(0) >= nv_ref[0])
    def _():
        o_ref[...] = jnp.zeros(o_ref.shape, F32)


def _expert_call(block_e, n_valid, xs, wg, wu, wd, layer):
    n_slots = xs.shape[0]
    tm = MOE_TILE
    nb = n_slots // tm

    def blk(b, be, nv):
        return jnp.minimum(b, nv[0] - 1)

    def wmap(b, be, nv):
        return (layer, be[blk(b, be, nv)], 0, 0)

    return pl.pallas_call(
        _expert_kernel,
        grid_spec=pltpu.PrefetchScalarGridSpec(
            num_scalar_prefetch=2, grid=(nb,),
            in_specs=[pl.BlockSpec((tm, D_MODEL // 2), lambda b, be, nv: (blk(b, be, nv), 0)),
                      pl.BlockSpec((None, None, D_MODEL, D_EXPERT), wmap),
                      pl.BlockSpec((None, None, D_MODEL, D_EXPERT), wmap),
                      pl.BlockSpec((None, None, D_EXPERT, D_MODEL), wmap)],
            out_specs=pl.BlockSpec((tm, D_MODEL), lambda b, be, nv: (b, 0))),
        out_shape=jax.ShapeDtypeStruct((n_slots, D_MODEL), F32),
        compiler_params=_params(("arbitrary",)),
        name="moe_experts",
    )(block_e, n_valid, xs, wg, wu, wd)


def _combine_kernel(dcur_ref, dnext_ref, o_ref, x_ref, gcol_ref, g2_ref, fg_ref, yc_ref, yl_ref, buf, sem,
                    *, tile, nsteps, ctx_tiles):
    i = pl.program_id(0)
    slot = i % 2

    def issue(dref, s):
        def body(t, carry):
            for k in range(2):
                pltpu.make_async_copy(o_ref.at[pl.ds(dref[k, t], 1)], buf.at[s, k, pl.ds(t, 1)], sem.at[s]).start()
            return carry
        lax.fori_loop(0, tile, body, 0, unroll=DMA_UNROLL)

    @pl.when(i == 0)
    def _():
        issue(dcur_ref, 0)

    @pl.when(i + 1 < nsteps)
    def _():
        issue(dnext_ref, 1 - slot)

    for k in range(2):
        pltpu.make_async_copy(o_ref.at[pl.ds(0, tile)], buf.at[slot, k], sem.at[slot]).wait()

    gc = gcol_ref[...]
    ffn = gc[:, 0:1] * buf[slot, 0] + gc[:, 1:2] * buf[slot, 1]
    x2 = x_ref[...] + g2_ref[...] * ffn
    y = x2 * lax.rsqrt(jnp.mean(x2 * x2, axis=-1, keepdims=True) + EPS) * fg_ref[...]

    @pl.when(i < ctx_tiles)
    def _():
        yc_ref[...] = y

    @pl.when(i >= ctx_tiles)
    def _():
        yl_ref[...] = y


def _combine_call(dest, o, x1, gcol, mod3, fg, n_ctx, seq_lat):
    n = x1.shape[0]
    tile = GATHER_TILE
    nsteps = n // tile
    ctx_tiles = n_ctx // tile
    mrow = functools.partial(_mod_row, tile=tile, n_ctx=n_ctx, seq_lat=seq_lat)
    out_specs = [pl.BlockSpec((tile, D_MODEL), lambda i: (jnp.minimum(i, ctx_tiles - 1), 0)),
                 pl.BlockSpec((tile, D_MODEL), lambda i: (jnp.maximum(i - ctx_tiles, 0), 0))]
    out_shape = [jax.ShapeDtypeStruct((n_ctx, D_MODEL), F32), jax.ShapeDtypeStruct((n - n_ctx, D_MODEL), F32)]
    return pl.pallas_call(
        functools.partial(_combine_kernel, tile=tile, nsteps=nsteps, ctx_tiles=ctx_tiles),
        grid=(nsteps,),
        in_specs=[pl.BlockSpec((2, tile), lambda i: (0, i), memory_space=pltpu.SMEM),
                  pl.BlockSpec((2, tile), lambda i: (0, jnp.minimum(i + 1, nsteps - 1)), memory_space=pltpu.SMEM),
                  pl.BlockSpec(memory_space=pl.ANY),
                  pl.BlockSpec((tile, D_MODEL), lambda i: (i, 0)),
                  pl.BlockSpec((tile, LANES), lambda i: (i, 0)),
                  pl.BlockSpec((None, 1, D_MODEL), lambda i: (mrow(i), 0, 5)),
                  pl.BlockSpec((1, D_MODEL), lambda i: (0, 0))],
        out_specs=out_specs,
        out_shape=out_shape,
        scratch_shapes=[pltpu.VMEM((2, 2, tile, D_MODEL), F32), pltpu.SemaphoreType.DMA((2,))],
        compiler_params=_params(("arbitrary",), disable_bounds_checks=True),
        name="moe_combine",
    )(dest, dest, o, x1, gcol, mod3, fg)


def _axial_tables(n, dim):
    t = jnp.arange(n)
    rows = (t // GRID_W).astype(F32)
    cols = (t % GRID_W).astype(F32)
    nf = dim // 4
    inv = jnp.power(ROPE_BASE, -jnp.arange(nf, dtype=F32) / nf)
    ang = jnp.concatenate([rows[:, None] * inv, cols[:, None] * inv], axis=-1)
    return jnp.cos(ang), jnp.sin(ang)


def _rope_lane_tables(n, dim):
    cos, sin = _axial_tables(n, dim)
    c = jnp.concatenate([cos, cos], axis=-1)
    s = jnp.concatenate([-sin, sin], axis=-1)
    pad = LANES - dim
    if pad:
        c = jnp.pad(c, ((0, 0), (0, pad)))
        s = jnp.pad(s, ((0, 0), (0, pad)))
    return c, s


def _identity_rows(n, dim):
    c = jnp.pad(jnp.ones((n, dim), F32), ((0, 0), (0, LANES - dim)))
    return c, jnp.zeros((n, LANES), F32)


def _swap_halves_cols(w):
    half = w.shape[-1] // 2
    return jnp.concatenate([w[..., half:], w[..., :half]], axis=-1)


def _moe_plan(eid, rank, counts, n_slots):
    tm = MOE_TILE
    counts = counts.astype(I32)
    padded = (counts + tm - 1) // tm * tm
    pend = jnp.cumsum(padded)
    pstart = pend - padded
    start_of = jnp.zeros(eid.shape, I32)
    for e in range(N_EXPERTS):
        start_of = jnp.where(eid == e, pstart[e], start_of)
    dest = (start_of + rank).astype(I32)
    n_valid = (pend[-1] // tm).astype(I32)
    blk_start = jnp.arange(n_slots // tm, dtype=I32) * tm
    block_e = jnp.minimum(jnp.sum((blk_start[:, None] >= pend[None, :]).astype(I32), axis=1), N_EXPERTS - 1)
    meta = jnp.stack([pstart + counts, padded - counts, jnp.broadcast_to(n_valid, counts.shape)]).astype(I32)
    return dest, block_e.astype(I32), n_valid.reshape(1), meta


def kernel(x_prompt, x_sample, c, cache_ckv, cache_krope, state_ret, c_ctx, w_ada, b_ada, norm_attn, norm_ffn,
           w_in, ret_decay_logit, q_norm, kv_norm, w_uq, w_ukv, w_out, w_router, b_router, w_exp_gate, w_exp_up,
           w_exp_down, final_norm):
    batch, seq, d = x_prompt.shape
    dbatch, dseq, _ = x_sample.shape
    depth = w_ada.shape[0]
    past = cache_ckv.shape[2]
    n_ctx, n_lat = batch * seq, dbatch * dseq
    n = n_ctx + n_lat
    assert d == D_MODEL and dbatch + 1 <= 8
    assert n_ctx % dseq == 0 and dseq % ROW_TILE == 0 and n_ctx % ROW_TILE == 0 and seq % RET_CHUNK == 0
    n_slots = (2 * n // MOE_TILE + N_EXPERTS) * MOE_TILE

    cond = jnp.concatenate([c_ctx[None, :], c, jnp.zeros((8 - 1 - dbatch, d), F32)], axis=0)
    mod = _ada_call(cond, w_ada, b_ada)

    cr, sr = _rope_lane_tables(dseq, D_RET)
    cm, sm = _rope_lane_tables(dseq, D_ROPE)
    one_q, zero_q = _identity_rows(ROW_TILE, D_ROPE)
    cos_q, sin_q = jnp.concatenate([one_q, cm], axis=0), jnp.concatenate([zero_q, sm], axis=0)
    one_p, zero_p = _identity_rows(past, D_ROPE)
    cos_k, sin_k = jnp.concatenate([one_p, cm], axis=0), jnp.concatenate([zero_p, sm], axis=0)
    one_c, zero_c = _identity_rows(seq, D_ROPE)
    eye = jnp.eye(LANES, dtype=BF16)

    wr = jnp.pad(w_router, ((0, 0), (0, LANES - N_EXPERTS))).astype(BF16)
    br = b_router.reshape(N_EXPERTS, 1).astype(F32)
    lg_all = jax.nn.log_sigmoid(ret_decay_logit.astype(F32))
    wg, wu, wd = w_exp_gate.astype(BF16), w_exp_up.astype(BF16), w_exp_down.astype(BF16)

    ckv_layers, krope_layers, state_layers = [], [], []
    pending = None
    for l in range(depth):
        mod3 = mod[l].reshape(8, 1, 6 * d)
        wl = w_in[l]
        cuts = 4 * RET_W + Q_RANK + KV_RANK
        w_in_p = jnp.concatenate(
            [wl, _swap_halves_cols(wl[:, cuts:]), jnp.zeros((d, IN_W - wl.shape[1] - D_ROPE), F32)],
            axis=1).astype(BF16)
        wq = w_uq[l].reshape(Q_RANK, H_MLA, D_NOPE + D_ROPE)
        w_uq_p = jnp.concatenate([wq, _swap_halves_cols(wq[..., D_NOPE:])], axis=-1)
        w_uq_p = w_uq_p.reshape(Q_RANK, H_MLA * Q_HEAD_W).astype(BF16)
        wkv = w_ukv[l].reshape(KV_RANK, H_MLA, D_NOPE + D_V)
        wkt = wkv[..., :D_NOPE].transpose(1, 2, 0).astype(BF16)
        wv = wkv[..., D_NOPE:].transpose(1, 0, 2).astype(BF16)
        wo = w_out[l].astype(BF16)

        if pending is None:
            x, qkvg, cqn, ckv, kr = _inproj_call(x_prompt.reshape(n_ctx, d), x_sample.reshape(n_lat, d), mod3,
                                                 norm_attn[l][None, :], w_in_p, q_norm[l][None, :],
                                                 kv_norm[l][None, :], dseq)
        else:
            x, qkvg, cqn, ckv, kr = _combine_inproj_call(*pending, mod3, norm_attn[l][None, :], w_in_p,
                                                         q_norm[l][None, :], kv_norm[l][None, :], n_ctx, dseq)
        ckv_layers.append(ckv[:n_ctx].reshape(batch, seq, KV_RANK))
        krope_layers.append(kr[:n_ctx, :D_ROPE].reshape(batch, seq, D_ROPE))

        ret_c, s_ctx = _retention_call(lg_all[l], qkvg, batch, seq, 0, H_RET, seq // RET_CHUNK, emit_state=True)
        (ret_l,) = _retention_call(lg_all[l], qkvg, dbatch, dseq, n_ctx // dseq, 2, 4, s0=state_ret[:, l],
                                   tables=(cr, sr))
        state_layers.append(s_ctx)

        q = _qproj_call(cqn, w_uq_p, cos_q, sin_q, n_ctx, dseq)
        kt_c, v_c = _kvproj_call(ckv[:n_ctx].reshape(batch, seq, KV_RANK).astype(BF16),
                                 kr[:n_ctx].reshape(batch, seq, LANES), one_c, zero_c, wkt, wv, eye, H_MLA)
        keys_ckv = jnp.concatenate([cache_ckv[:, l], ckv[n_ctx:].reshape(dbatch, dseq, KV_RANK)], axis=1).astype(BF16)
        keys_kr = jnp.concatenate([jnp.pad(cache_krope[:, l], ((0, 0), (0, 0), (0, LANES - D_ROPE))),
                                   kr[n_ctx:].reshape(dbatch, dseq, LANES)], axis=1)
        kt_l, v_l = _kvproj_call(keys_ckv, keys_kr, cos_k, sin_k, wkt, wv, eye, 2)
        mla_c = _attn_call(q, kt_c, v_c, seq, 0, seq, H_MLA)
        mla_l = _attn_call(q, kt_l, v_l, dseq, n_ctx, Q_TILE, 4)

        x1, hp, eid, rank, gcol, cnt = _outproj_call(x, (ret_c, ret_l), (mla_c, mla_l), wo, mod3,
                                                     norm_ffn[l][None, :], wr, br, n_ctx, dseq)

        dest, block_e, n_valid, meta = _moe_plan(eid, rank, cnt[:, 0], n_slots)
        xs = _dispatch_call(meta, dest, hp, n_slots)
        o = _expert_call(block_e, n_valid, xs, wg, wu, wd, l)
        pending = (dest, o, x1, gcol, mod3)

    y_ctx, y_lat = _combine_call(*pending, final_norm[None, :], n_ctx, dseq)
    return (y_ctx.reshape(batch, seq, d), y_lat.reshape(dbatch, dseq, d), jnp.stack(ckv_layers, axis=1),
            jnp.stack(krope_layers, axis=1), jnp.stack(state_layers, axis=1))
```

```python
import functools
import math

import jax
import jax.numpy as jnp
from jax import lax
from jax.experimental import pallas as pl
from jax.experimental.pallas import tpu as pltpu

F32 = jnp.float32
BF16 = jnp.bfloat16
I32 = jnp.int32

D_MODEL = 2048
GRID_W = 64
H_RET = 8
D_RET = 128
RET_W = H_RET * D_RET
RET_CHUNK = 128
H_MLA = 8
D_NOPE = 128
D_ROPE = 64
D_V = 128
Q_RANK = 512
KV_RANK = 256
N_EXPERTS = 16
N_GROUPS = 4
EXPERTS_PER_GROUP = N_EXPERTS // N_GROUPS
D_EXPERT = 1024
ROPE_BASE = 10000.0
EPS = 1e-6

LANES = 128
Q_HEAD_W = 2 * LANES
LAT_W = 1024
IN_W = 4 * RET_W + LAT_W
ROW_TILE = 512
OUT_TILE = 256
Q_TILE = 256
MOE_TILE = 256
GATHER_TILE = 256
DMA_UNROLL = 8
VMEM_LIMIT = 56 * 1024 * 1024


def _params(sem, **kw):
    return pltpu.CompilerParams(dimension_semantics=sem, vmem_limit_bytes=VMEM_LIMIT, **kw)


def _silu(x):
    return x * jax.nn.sigmoid(x)


def _mod_row(i, tile, n_ctx, seq_lat):
    r0 = i * tile
    return jnp.where(r0 < n_ctx, 0, 1 + (r0 - n_ctx) // seq_lat)


def _pos_block(i, tile, n_ctx, seq_lat):
    r0 = i * tile
    return jnp.where(r0 < n_ctx, 0, 1 + ((r0 - n_ctx) % seq_lat) // tile)


def _ada_kernel(c_ref, w_ref, b_ref, o_ref):
    a = _silu(c_ref[...]).astype(BF16)
    o_ref[...] = jnp.dot(a, w_ref[...].astype(BF16), preferred_element_type=F32) + b_ref[...]


def _ada_call(cond, w_ada, b_ada):
    depth, d, n6 = w_ada.shape
    tn = 1536
    return pl.pallas_call(
        _ada_kernel,
        grid=(depth, n6 // tn),
        in_specs=[pl.BlockSpec((8, d), lambda l, j: (0, 0)),
                  pl.BlockSpec((None, d, tn), lambda l, j: (l, 0, j)),
                  pl.BlockSpec((None, 1, tn), lambda l, j: (l, 0, j))],
        out_specs=pl.BlockSpec((None, 8, tn), lambda l, j: (l, 0, j)),
        out_shape=jax.ShapeDtypeStruct((depth, 8, n6), F32),
        compiler_params=_params(("arbitrary", "arbitrary")),
        name="adaln",
    )(cond, w_ada, b_ada.reshape(depth, 1, n6))


def _inproj_kernel(xc_ref, xl_ref, sh_ref, sc_ref, ng_ref, w_ref, qg_ref, kvg_ref, x_ref, qkvg_ref, cq_ref,
                   ckv_ref, kr_ref, *, ctx_tiles):
    x = jnp.where(pl.program_id(0) < ctx_tiles, xc_ref[...], xl_ref[...])
    x_ref[...] = x
    _project_rows(x, sh_ref, sc_ref, ng_ref, w_ref, qg_ref, kvg_ref, qkvg_ref, cq_ref, ckv_ref, kr_ref,
                  slice(None))


def _inproj_call(x_ctx, x_lat, mod3, ng, w_in_p, qg, kvg, seq_lat):
    n_ctx = x_ctx.shape[0]
    n = n_ctx + x_lat.shape[0]
    tm = GATHER_TILE
    ctx_tiles = n_ctx // tm
    mrow = functools.partial(_mod_row, tile=tm, n_ctx=n_ctx, seq_lat=seq_lat)
    return pl.pallas_call(
        functools.partial(_inproj_kernel, ctx_tiles=ctx_tiles),
        grid=(n // tm,),
        in_specs=[pl.BlockSpec((tm, D_MODEL), lambda i: (jnp.minimum(i, ctx_tiles - 1), 0)),
                  pl.BlockSpec((tm, D_MODEL), lambda i: (jnp.maximum(i - ctx_tiles, 0), 0)),
                  pl.BlockSpec((None, 1, D_MODEL), lambda i: (mrow(i), 0, 0)),
                  pl.BlockSpec((None, 1, D_MODEL), lambda i: (mrow(i), 0, 1)),
                  pl.BlockSpec((1, D_MODEL), lambda i: (0, 0)),
                  pl.BlockSpec((D_MODEL, IN_W), lambda i: (0, 0), pipeline_mode=pl.Buffered(1)),
                  pl.BlockSpec((1, Q_RANK), lambda i: (0, 0)),
                  pl.BlockSpec((1, KV_RANK), lambda i: (0, 0))],
        out_specs=[pl.BlockSpec((tm, D_MODEL), lambda i: (i, 0)),
                   pl.BlockSpec((tm, 4 * RET_W), lambda i: (i, 0)),
                   pl.BlockSpec((tm, Q_RANK), lambda i: (i, 0)),
                   pl.BlockSpec((tm, KV_RANK), lambda i: (i, 0)),
                   pl.BlockSpec((tm, LANES), lambda i: (i, 0))],
        out_shape=[jax.ShapeDtypeStruct((n, D_MODEL), F32),
                   jax.ShapeDtypeStruct((n, 4 * RET_W), BF16),
                   jax.ShapeDtypeStruct((n, Q_RANK), BF16),
                   jax.ShapeDtypeStruct((n, KV_RANK), F32),
                   jax.ShapeDtypeStruct((n, LANES), F32)],
        compiler_params=_params(("arbitrary",)),
        name="inproj",
    )(x_ctx, x_lat, mod3, mod3, ng, w_in_p, qg, kvg)


def _project_rows(x, sh_ref, sc_ref, ng_ref, w_ref, qg_ref, kvg_ref, qkvg_ref, cq_ref, ckv_ref, kr_ref, rs,
                  after_chunk=lambda j: None):
    y = x * lax.rsqrt(jnp.mean(x * x, axis=-1, keepdims=True) + EPS) * ng_ref[...]
    h = (y * (1.0 + sc_ref[...]) + sh_ref[...]).astype(BF16)
    for j in range(4 * RET_W // LAT_W):
        cs = slice(j * LAT_W, (j + 1) * LAT_W)
        qkvg_ref[rs, cs] = jnp.dot(h, w_ref[:, cs], preferred_element_type=F32).astype(BF16)
        after_chunk(j)
    acc = jnp.dot(h, w_ref[:, 4 * RET_W:], preferred_element_type=F32)
    cq = acc[:, :Q_RANK]
    cq = cq * lax.rsqrt(jnp.mean(cq * cq, axis=-1, keepdims=True) + EPS) * qg_ref[...]
    cq_ref[rs, :] = cq.astype(BF16)
    ckv = acc[:, Q_RANK:Q_RANK + KV_RANK]
    ckv_ref[rs, :] = ckv * lax.rsqrt(jnp.mean(ckv * ckv, axis=-1, keepdims=True) + EPS) * kvg_ref[...]
    kr_ref[rs, :] = acc[:, Q_RANK + KV_RANK:Q_RANK + KV_RANK + LANES]


def _combine_inproj_kernel(dcur_ref, dnext_ref, o_ref, x1_ref, gcol_ref, g2_ref, sh_ref, sc_ref, ng_ref, w_ref,
                           qg_ref, kvg_ref, x2_ref, qkvg_ref, cq_ref, ckv_ref, kr_ref, buf, sem, *, tile, nsteps):
    i = pl.program_id(0)
    slot = i % 2

    def row_copy(dref, s, k, t):
        return pltpu.make_async_copy(o_ref.at[pl.ds(dref[k, t], 1)], buf.at[s, k, pl.ds(t, 1)], sem.at[s])

    @pl.when(i == 0)
    def _():
        def body(t, carry):
            for k in range(2):
                row_copy(dcur_ref, 0, k, t).start()
            return carry
        lax.fori_loop(0, tile, body, 0, unroll=DMA_UNROLL)

    for k in range(2):
        pltpu.make_async_copy(o_ref.at[pl.ds(0, tile)], buf.at[slot, k], sem.at[slot]).wait()

    gc = gcol_ref[...]
    x2 = x1_ref[...] + g2_ref[...] * (gc[:, 0:1] * buf[slot, 0] + gc[:, 1:2] * buf[slot, 1])
    x2_ref[...] = x2

    nchunks = 4 * RET_W // LAT_W
    share = tile // nchunks

    def request_rows(j):
        for t in range(j * share, (j + 1) * share):
            for k in range(2):
                row_copy(dnext_ref, 1 - slot, k, t).start()

    _project_rows(x2, sh_ref, sc_ref, ng_ref, w_ref, qg_ref, kvg_ref, qkvg_ref, cq_ref, ckv_ref, kr_ref,
                  slice(None), after_chunk=request_rows)

    @pl.when(i == nsteps - 1)
    def _():
        for k in range(2):
            pltpu.make_async_copy(o_ref.at[pl.ds(0, tile)], buf.at[1 - slot, k], sem.at[1 - slot]).wait()


def _combine_inproj_call(dest, o, x1, gcol, mod_prev, mod3, ng, w_in_p, qg, kvg, n_ctx, seq_lat):
    n = x1.shape[0]
    tile = GATHER_TILE
    nsteps = n // tile
    mrow = functools.partial(_mod_row, tile=tile, n_ctx=n_ctx, seq_lat=seq_lat)

    def rows(width):
        return pl.BlockSpec((tile, width), lambda i: (i, 0))

    def const(shape):
        return pl.BlockSpec(shape, lambda i: (0, 0))

    return pl.pallas_call(
        functools.partial(_combine_inproj_kernel, tile=tile, nsteps=nsteps),
        grid=(nsteps,),
        in_specs=[pl.BlockSpec((2, tile), lambda i: (0, i), memory_space=pltpu.SMEM),
                  pl.BlockSpec((2, tile), lambda i: (0, jnp.minimum(i + 1, nsteps - 1)), memory_space=pltpu.SMEM),
                  pl.BlockSpec(memory_space=pl.ANY),
                  rows(D_MODEL), rows(LANES),
                  pl.BlockSpec((None, 1, D_MODEL), lambda i: (mrow(i), 0, 5)),
                  pl.BlockSpec((None, 1, D_MODEL), lambda i: (mrow(i), 0, 0)),
                  pl.BlockSpec((None, 1, D_MODEL), lambda i: (mrow(i), 0, 1)),
                  const((1, D_MODEL)),
                  pl.BlockSpec((D_MODEL, IN_W), lambda i: (0, 0), pipeline_mode=pl.Buffered(1)),
                  const((1, Q_RANK)), const((1, KV_RANK))],
        out_specs=[rows(D_MODEL), rows(4 * RET_W), rows(Q_RANK), rows(KV_RANK), rows(LANES)],
        out_shape=[jax.ShapeDtypeStruct((n, D_MODEL), F32),
                   jax.ShapeDtypeStruct((n, 4 * RET_W), BF16),
                   jax.ShapeDtypeStruct((n, Q_RANK), BF16),
                   jax.ShapeDtypeStruct((n, KV_RANK), F32),
                   jax.ShapeDtypeStruct((n, LANES), F32)],
        scratch_shapes=[pltpu.VMEM((2, 2, tile, D_MODEL), F32), pltpu.SemaphoreType.DMA((2,))],
        compiler_params=_params(("arbitrary",), disable_bounds_checks=True),
        name="combine_inproj",
    )(dest, dest, o, x1, gcol, mod_prev, mod3, mod3, ng, w_in_p, qg, kvg)


def _retention_kernel(lg_ref, *refs, n, hb, unroll, rope, has_s0, emit_state):
    it = iter(refs)
    q_ref, k_ref, v_ref, g_ref = next(it), next(it), next(it), next(it)
    s0_ref = next(it) if has_s0 else None
    cos_ref, sin_ref = (next(it), next(it)) if rope else (None, None)
    o_ref = next(it)
    sfin_ref = next(it) if emit_state else None
    kvf_scr, kvb_scr, kt_scr, dm_scr = next(it), next(it), next(it), next(it)

    C = RET_CHUNK
    nc = n // C
    hblk = pl.program_id(1)
    diff = (lax.broadcasted_iota(I32, (C, C), 0) - lax.broadcasted_iota(I32, (C, C), 1)).astype(F32)
    col = lax.broadcasted_iota(I32, (C, 1), 0).astype(F32)
    lane = lax.broadcasted_iota(I32, (1, C), 1).astype(F32)
    k_scale = D_RET ** -0.5

    xi_f, xi_b, zeta_f, zeta_b, g_f, g_b = [], [], [], [], [], []
    for hh in range(hb):
        lgf = lg_ref[0, hblk * hb + hh]
        lgb = lg_ref[1, hblk * hb + hh]
        dm_scr[hh] = (jnp.where(diff >= 0, jnp.exp(jnp.maximum(diff, 0.0) * lgf), 0.0)
                      + jnp.where(diff <= 0, jnp.exp(jnp.maximum(-diff, 0.0) * lgb), 0.0))
        xi_f.append(jnp.exp((col + 1.0) * lgf))
        xi_b.append(jnp.exp((C - col) * lgb))
        zeta_f.append(jnp.exp((C - 1.0 - lane) * lgf))
        zeta_b.append(jnp.exp(lane * lgb))
        g_f.append(jnp.exp(jnp.full((1, C), float(C), F32) * lgf))
        g_b.append(jnp.exp(jnp.full((1, C), float(C), F32) * lgb))

    def rows(c):
        return pl.ds(pl.multiple_of(c * C, C), C)

    def cols(hh):
        return slice(hh * C, (hh + 1) * C)

    def rot(x, c):
        if not rope:
            return x
        return x * cos_ref[rows(c), :] + pltpu.roll(x, C // 2, 1) * sin_ref[rows(c), :]

    def phase_a(c, carry):
        for hh in range(hb):
            kt = rot(k_ref[rows(c), cols(hh)].astype(F32) * k_scale, c).T
            kt_scr[hh, c] = kt.astype(BF16)
            vc = v_ref[rows(c), cols(hh)]
            kvf_scr[hh, c] = jnp.dot((kt * zeta_f[hh]).astype(BF16), vc, preferred_element_type=F32)
            kvb_scr[hh, c] = jnp.dot((kt * zeta_b[hh]).astype(BF16), vc, preferred_element_type=F32)
        return carry

    lax.fori_loop(0, nc, phase_a, 0, unroll=unroll)

    for hh in range(hb):
        def scan(i, carry, hh=hh):
            sf, sb = carry
            cb = nc - 1 - i
            upd_f = kvf_scr[hh, i]
            kvf_scr[hh, i] = sf
            upd_b = kvb_scr[hh, cb]
            kvb_scr[hh, cb] = sb
            return g_f[hh] * sf + upd_f, g_b[hh] * sb + upd_b

        if has_s0:
            init = (s0_ref[0, hh], s0_ref[1, hh])
        else:
            init = (jnp.zeros((C, C), F32), jnp.zeros((C, C), F32))
        s_f, s_b = lax.fori_loop(0, nc, scan, init)
        if emit_state:
            sfin_ref[0, hh] = s_f
            sfin_ref[1, hh] = s_b

    def phase_c(c, carry):
        for hh in range(hb):
            qc = rot(q_ref[rows(c), cols(hh)].astype(F32), c)
            inner = jnp.dot(qc.astype(BF16), kt_scr[hh, c], preferred_element_type=F32) * dm_scr[hh]
            o = (jnp.dot(inner.astype(BF16), v_ref[rows(c), cols(hh)], preferred_element_type=F32)
                 + jnp.dot((qc * xi_f[hh]).astype(BF16), kvf_scr[hh, c].astype(BF16), preferred_element_type=F32)
                 + jnp.dot((qc * xi_b[hh]).astype(BF16), kvb_scr[hh, c].astype(BF16), preferred_element_type=F32))
            dev = o - jnp.mean(o, axis=-1, keepdims=True)
            on = dev * lax.rsqrt(jnp.mean(dev * dev, axis=-1, keepdims=True) + EPS)
            gate = g_ref[rows(c), cols(hh)].astype(F32)
            o_ref[rows(c), cols(hh)] = (on * _silu(gate)).astype(BF16)
        return carry

    lax.fori_loop(0, nc, phase_c, 0, unroll=unroll)


def _retention_call(lg2, qkvg, batch, n, row_blk0, hb, unroll, *, s0=None, tables=None, emit_state=False):
    hblks = H_RET // hb
    w = hb * D_RET
    nc = n // RET_CHUNK

    def qspec(part):
        return pl.BlockSpec((n, w), lambda b, h, lg: (row_blk0 + b, part * hblks + h))

    in_specs = [qspec(0), qspec(1), qspec(2), qspec(3)]
    args = [qkvg, qkvg, qkvg, qkvg]
    if s0 is not None:
        in_specs.append(pl.BlockSpec((None, 2, hb, D_RET, D_RET), lambda b, h, lg: (b, 0, h, 0, 0)))
        args.append(s0)
    if tables is not None:
        in_specs += [pl.BlockSpec((n, D_RET), lambda b, h, lg: (0, 0))] * 2
        args += list(tables)
    out_specs = [pl.BlockSpec((n, w), lambda b, h, lg: (b, h))]
    out_shape = [jax.ShapeDtypeStruct((batch * n, RET_W), BF16)]
    if emit_state:
        out_specs.append(pl.BlockSpec((None, 2, hb, D_RET, D_RET), lambda b, h, lg: (b, 0, h, 0, 0)))
        out_shape.append(jax.ShapeDtypeStruct((batch, 2, H_RET, D_RET, D_RET), F32))
    kern = functools.partial(_retention_kernel, n=n, hb=hb, unroll=unroll, rope=tables is not None,
                             has_s0=s0 is not None, emit_state=emit_state)
    return pl.pallas_call(
        kern,
        grid_spec=pltpu.PrefetchScalarGridSpec(
            num_scalar_prefetch=1, grid=(batch, hblks), in_specs=in_specs, out_specs=out_specs,
            scratch_shapes=[pltpu.VMEM((hb, nc, D_RET, D_RET), F32), pltpu.VMEM((hb, nc, D_RET, D_RET), F32),
                            pltpu.VMEM((hb, nc, D_RET, D_RET), BF16), pltpu.VMEM((hb, D_RET, D_RET), F32)]),
        out_shape=out_shape,
        compiler_params=_params(("arbitrary", "arbitrary")),
        name="retention",
    )(lg2, *args)


def _qproj_kernel(cq_ref, w_ref, cos_ref, sin_ref, q_ref):
    scale = (D_NOPE + D_ROPE) ** -0.5 * math.log2(math.e)
    acc = jnp.dot(cq_ref[...], w_ref[...], preferred_element_type=F32)
    for h in range(H_MLA):
        base = h * Q_HEAD_W
        q_ref[:, base:base + LANES] = (acc[:, base:base + LANES] * scale).astype(BF16)
        tail = acc[:, base + LANES:base + Q_HEAD_W]
        rot = tail * cos_ref[...] + pltpu.roll(tail, D_ROPE, 1) * sin_ref[...]
        q_ref[:, base + LANES:base + Q_HEAD_W] = (rot * scale).astype(BF16)


def _qproj_call(cqn, w_uq_p, cos_q, sin_q, n_ctx, seq_lat):
    n = cqn.shape[0]
    tm = ROW_TILE
    pblk = functools.partial(_pos_block, tile=tm, n_ctx=n_ctx, seq_lat=seq_lat)
    return pl.pallas_call(
        _qproj_kernel,
        grid=(n // tm,),
        in_specs=[pl.BlockSpec((tm, Q_RANK), lambda i: (i, 0)),
                  pl.BlockSpec((Q_RANK, H_MLA * Q_HEAD_W), lambda i: (0, 0)),
                  pl.BlockSpec((tm, LANES), lambda i: (pblk(i), 0)),
                  pl.BlockSpec((tm, LANES), lambda i: (pblk(i), 0))],
        out_specs=pl.BlockSpec((tm, H_MLA * Q_HEAD_W), lambda i: (i, 0)),
        out_shape=jax.ShapeDtypeStruct((n, H_MLA * Q_HEAD_W), BF16),
        compiler_params=_params(("arbitrary",)),
        name="qproj",
    )(cqn, w_uq_p, cos_q, sin_q)


def _kvproj_kernel(*refs, hb, past):
    if past:
        cckv_ref, ckr_ref, ckv_ref, kr_ref, cos_ref, sin_ref, wkt_ref, wv_ref, eye_ref, kt_ref, v_ref = refs
        parts = [(0, past, cckv_ref, ckr_ref), (past, kt_ref.shape[-1], ckv_ref, kr_ref)]
    else:
        ckv_ref, kr_ref, cos_ref, sin_ref, wkt_ref, wv_ref, eye_ref, kt_ref, v_ref = refs
        parts = [(0, kt_ref.shape[-1], ckv_ref, kr_ref)]
    nt = (((1,), (1,)), ((), ()))
    for lo, hi, c_ref, r_ref in parts:
        keys = c_ref[...].astype(BF16)
        kr = r_ref[...]
        rot = (kr * cos_ref[lo:hi, :] + pltpu.roll(kr, D_ROPE, 1) * sin_ref[lo:hi, :]).astype(BF16)
        rope_t = lax.dot_general(eye_ref[...], rot, nt, preferred_element_type=F32).astype(BF16)
        for hh in range(hb):
            kt_ref[hh, :D_NOPE, lo:hi] = lax.dot_general(wkt_ref[hh], keys, nt,
                                                         preferred_element_type=F32).astype(BF16)
            kt_ref[hh, D_NOPE:, lo:hi] = rope_t
            v_ref[hh, lo:hi, :] = jnp.dot(keys, wv_ref[hh], preferred_element_type=F32).astype(BF16)


def _kvproj_call(ckv, kr, batch, n, row_blk0, cos_k, sin_k, wkt, wv, eye, hb, cache=None):
    hblks = H_MLA // hb
    past = 0 if cache is None else cache[0].shape[2]
    nk = past + n
    in_specs, args = [], []
    if cache is not None:
        layer = cache[2]
        in_specs += [pl.BlockSpec((None, None, past, KV_RANK), lambda b, h: (b, layer, 0, 0)),
                     pl.BlockSpec((None, None, past, LANES), lambda b, h: (b, layer, 0, 0))]
        args += [cache[0], cache[1]]
    in_specs += [pl.BlockSpec((n, KV_RANK), lambda b, h: (row_blk0 + b, 0)),
                 pl.BlockSpec((n, LANES), lambda b, h: (row_blk0 + b, 0)),
                 pl.BlockSpec((nk, LANES), lambda b, h: (0, 0)),
                 pl.BlockSpec((nk, LANES), lambda b, h: (0, 0)),
                 pl.BlockSpec((hb, D_NOPE, KV_RANK), lambda b, h: (h, 0, 0)),
                 pl.BlockSpec((hb, KV_RANK, D_V), lambda b, h: (h, 0, 0)),
                 pl.BlockSpec((LANES, LANES), lambda b, h: (0, 0))]
    args += [ckv, kr, cos_k, sin_k, wkt, wv, eye]
    return pl.pallas_call(
        functools.partial(_kvproj_kernel, hb=hb, past=past),
        grid=(batch, hblks),
        in_specs=in_specs,
        out_specs=[pl.BlockSpec((None, hb, Q_HEAD_W, nk), lambda b, h: (b, h, 0, 0)),
                   pl.BlockSpec((None, hb, nk, D_V), lambda b, h: (b, h, 0, 0))],
        out_shape=[jax.ShapeDtypeStruct((batch, H_MLA, Q_HEAD_W, nk), BF16),
                   jax.ShapeDtypeStruct((batch, H_MLA, nk, D_V), BF16)],
        compiler_params=_params(("arbitrary", "arbitrary")),
        name="kvproj",
    )(*args)


def _attn_kernel(q_ref, kt_ref, v_ref, o_ref, *, hb):
    for hh in range(hb):
        q = q_ref[:, hh * Q_HEAD_W:(hh + 1) * Q_HEAD_W]
        s = jnp.dot(q, kt_ref[hh], preferred_element_type=F32)
        p = jnp.exp2(s - jnp.max(s, axis=-1, keepdims=True))
        den = jnp.sum(p, axis=-1, keepdims=True)
        o = jnp.dot(p.astype(BF16), v_ref[hh], preferred_element_type=F32)
        o_ref[:, hh * D_V:(hh + 1) * D_V] = (o / den).astype(BF16)


def _attn_call(q, kt, v, n, row0, tq, hb):
    batch, _, _, nk = kt.shape
    hblks = H_MLA // hb
    nq = n // tq
    blk0 = row0 // tq
    return pl.pallas_call(
        functools.partial(_attn_kernel, hb=hb),
        grid=(batch, hblks, nq),
        in_specs=[pl.BlockSpec((tq, hb * Q_HEAD_W), lambda b, h, i: (blk0 + b * nq + i, h)),
                  pl.BlockSpec((None, hb, Q_HEAD_W, nk), lambda b, h, i: (b, h, 0, 0)),
                  pl.BlockSpec((None, hb, nk, D_V), lambda b, h, i: (b, h, 0, 0))],
        out_specs=pl.BlockSpec((tq, hb * D_V), lambda b, h, i: (b * nq + i, h)),
        out_shape=jax.ShapeDtypeStruct((batch * n, H_MLA * D_V), BF16),
        compiler_params=_params(("arbitrary", "arbitrary", "arbitrary")),
        name="mla_attention",
    )(q, kt, v)


def _route_rows(s_rows, sel_rows):
    epg = EXPERTS_PER_GROUP
    best_g = best_v = None
    for g in range(N_GROUPS):
        v = sel_rows[g * epg:(g + 1) * epg]
        top2 = None
        for a in range(epg):
            for b in range(a + 1, epg):
                pair = v[a] + v[b]
                top2 = pair if top2 is None else jnp.maximum(top2, pair)
        if g == 0:
            best_v, best_g = top2, jnp.zeros(top2.shape, I32)
        else:
            upd = top2 > best_v
            best_v = jnp.where(upd, top2, best_v)
            best_g = jnp.where(upd, g, best_g)

    def pick(rows_, i):
        out = rows_[i]
        for g in range(1, N_GROUPS):
            out = jnp.where(best_g == g, rows_[g * epg + i], out)
        return out

    w = [pick(sel_rows, i) for i in range(epg)]
    sv = [pick(s_rows, i) for i in range(epg)]
    m0, i0, s0 = w[0], jnp.zeros(w[0].shape, I32), sv[0]
    for i in range(1, epg):
        upd = w[i] > m0
        m0, i0, s0 = jnp.where(upd, w[i], m0), jnp.where(upd, i, i0), jnp.where(upd, sv[i], s0)
    m1 = jnp.full(w[0].shape, -jnp.inf, F32)
    i1, s1 = jnp.zeros(w[0].shape, I32), sv[0]
    for i in range(epg):
        upd = (i0 != i) & (w[i] > m1)
        m1, i1, s1 = jnp.where(upd, w[i], m1), jnp.where(upd, i, i1), jnp.where(upd, sv[i], s1)
    tot = s0 + s1
    return best_g * epg + i0, best_g * epg + i1, s0 / tot, s1 / tot


def _outproj_kernel(x_ref, mrc_ref, mrl_ref, mmc_ref, mml_ref, wo_ref, g1_ref, sc_ref, sh_ref, ng_ref, wr_ref,
                    br_ref, x1_ref, hp_ref, eid_ref, rank_ref, gcol_ref, cnt_ref, *, ctx_tiles):
    half = D_MODEL // 2

    @pl.when(pl.program_id(0) == 0)
    def _():
        cnt_ref[...] = jnp.zeros(cnt_ref.shape, F32)

    is_ctx = pl.program_id(0) < ctx_tiles
    mr = jnp.where(is_ctx, mrc_ref[...], mrl_ref[...])
    mm = jnp.where(is_ctx, mmc_ref[...], mml_ref[...])
    mix = (jnp.dot(mr, wo_ref[:RET_W, :], preferred_element_type=F32)
           + jnp.dot(mm, wo_ref[RET_W:, :], preferred_element_type=F32))
    x1 = x_ref[...] + g1_ref[...] * mix
    x1_ref[...] = x1
    y = x1 * lax.rsqrt(jnp.mean(x1 * x1, axis=-1, keepdims=True) + EPS) * ng_ref[...]
    h2 = (y * (1.0 + sc_ref[...]) + sh_ref[...]).astype(BF16)
    hi = lax.bitcast_convert_type(h2[:, :half].astype(F32), I32)
    lo = lax.bitcast_convert_type(h2[:, half:].astype(F32), I32)
    hp_ref[...] = (hi & jnp.int32(-65536)) | lax.shift_right_logical(lo, 16)

    logits = jnp.dot(h2, wr_ref[...], preferred_element_type=F32)
    lt = logits.T
    s_rows = [jax.nn.sigmoid(lt[e:e + 1, :]) for e in range(N_EXPERTS)]
    sel_rows = [s_rows[e] + br_ref[e:e + 1, :] for e in range(N_EXPERTS)]
    e0, e1, g0, g1 = _route_rows(s_rows, sel_rows)
    eid_ref[0:1, :] = e0
    eid_ref[1:2, :] = e1
    t = g0.shape[1]
    erow = lax.broadcasted_iota(I32, (N_EXPERTS, t), 0)
    hit0 = erow == e0
    hit1 = erow == e1
    onehot = jnp.where(hit0 | hit1, 1.0, 0.0)
    before = lax.broadcasted_iota(I32, (t, t), 0) < lax.broadcasted_iota(I32, (t, t), 1)
    prefix = jnp.dot(onehot.astype(BF16), jnp.where(before, 1.0, 0.0).astype(BF16), preferred_element_type=F32)
    seen = prefix + cnt_ref[:, 0:1]
    rank_ref[0:1, :] = jnp.sum(jnp.where(hit0, seen, 0.0), axis=0, keepdims=True).astype(I32)
    rank_ref[1:2, :] = jnp.sum(jnp.where(hit1, seen, 0.0), axis=0, keepdims=True).astype(I32)
    cnt_ref[...] = cnt_ref[...] + jnp.sum(onehot, axis=1, keepdims=True)
    row = lax.broadcasted_iota(I32, (LANES, t), 0)
    gcol_ref[...] = jnp.where(row == 0, g0, jnp.where(row == 1, g1, 0.0)).T


def _outproj_call(x, mix_ret, mix_mla, wo, mod3, ng, wr, br, n_ctx, seq_lat):
    n = x.shape[0]
    tm = OUT_TILE
    ctx_tiles = n_ctx // tm
    mrow = functools.partial(_mod_row, tile=tm, n_ctx=n_ctx, seq_lat=seq_lat)

    def mspec(chunk):
        return pl.BlockSpec((None, 1, D_MODEL), lambda i: (mrow(i), 0, chunk))

    def ctx_spec(width):
        return pl.BlockSpec((tm, width), lambda i: (jnp.minimum(i, ctx_tiles - 1), 0))

    def lat_spec(width):
        return pl.BlockSpec((tm, width), lambda i: (jnp.maximum(i - ctx_tiles, 0), 0))

    return pl.pallas_call(
        functools.partial(_outproj_kernel, ctx_tiles=ctx_tiles),
        grid=(n // tm,),
        in_specs=[pl.BlockSpec((tm, D_MODEL), lambda i: (i, 0)),
                  ctx_spec(RET_W), lat_spec(RET_W), ctx_spec(H_MLA * D_V), lat_spec(H_MLA * D_V),
                  pl.BlockSpec((D_MODEL, D_MODEL), lambda i: (0, 0)),
                  mspec(2), mspec(4), mspec(3),
                  pl.BlockSpec((1, D_MODEL), lambda i: (0, 0)),
                  pl.BlockSpec((D_MODEL, LANES), lambda i: (0, 0)),
                  pl.BlockSpec((N_EXPERTS, 1), lambda i: (0, 0))],
        out_specs=[pl.BlockSpec((tm, D_MODEL), lambda i: (i, 0)),
                   pl.BlockSpec((tm, D_MODEL // 2), lambda i: (i, 0)),
                   pl.BlockSpec((2, tm), lambda i: (0, i)),
                   pl.BlockSpec((2, tm), lambda i: (0, i)),
                   pl.BlockSpec((tm, LANES), lambda i: (i, 0)),
                   pl.BlockSpec((N_EXPERTS, LANES), lambda i: (0, 0))],
        out_shape=[jax.ShapeDtypeStruct((n, D_MODEL), F32),
                   jax.ShapeDtypeStruct((n, D_MODEL // 2), I32),
                   jax.ShapeDtypeStruct((2, n), I32),
                   jax.ShapeDtypeStruct((2, n), I32),
                   jax.ShapeDtypeStruct((n, LANES), F32),
                   jax.ShapeDtypeStruct((N_EXPERTS, LANES), F32)],
        compiler_params=_params(("arbitrary",)),
        name="outproj_router",
    )(x, mix_ret[0], mix_ret[1], mix_mla[0], mix_mla[1], wo, mod3, mod3, mod3, ng, wr, br)


def _dispatch_kernel(meta_ref, dest_ref, src_ref, out_ref, sem, *, tile, nsteps, nblocks):
    def row_to(t, slot, s):
        return pltpu.make_async_copy(src_ref.at[pl.ds(t, 1)], out_ref.at[pl.ds(slot, 1)], s)

    def issue(t, carry):
        row_to(t, dest_ref[0, t], sem.at[0]).start()
        row_to(t, dest_ref[1, t], sem.at[0]).start()
        return carry

    lax.fori_loop(0, tile, issue, 0, unroll=DMA_UNROLL)
    for _ in range(2):
        pltpu.make_async_copy(src_ref, out_ref.at[pl.ds(0, tile)], sem.at[0]).wait()

    @pl.when(pl.program_id(0) == nsteps - 1)
    def _():
        for e in range(N_EXPERTS):
            first = meta_ref[0, e]
            npad = meta_ref[1, e]

            def fill(r, carry):
                row_to(0, first + r, sem.at[1]).start()
                return carry

            lax.fori_loop(0, npad, fill, 0)

            def fdrain(r, carry):
                row_to(0, 0, sem.at[1]).wait()
                return carry

            lax.fori_loop(0, npad, fdrain, 0)

        def tail_copy(b):
            return pltpu.make_async_copy(src_ref, out_ref.at[pl.ds(pl.multiple_of(b * tile, tile), tile)], sem.at[1])

        def tail(b, carry):
            tail_copy(b).start()
            tail_copy(b).wait()
            return carry

        lax.fori_loop(meta_ref[2, 0], nblocks, tail, 0)


def _dispatch_call(meta, dest, hp, n_slots):
    n, wcols = hp.shape
    tile = GATHER_TILE
    nsteps = n // tile
    return pl.pallas_call(
        functools.partial(_dispatch_kernel, tile=tile, nsteps=nsteps, nblocks=n_slots // tile),
        grid_spec=pltpu.PrefetchScalarGridSpec(
            num_scalar_prefetch=1, grid=(nsteps,),
            in_specs=[pl.BlockSpec((2, tile), lambda i, m: (0, i), memory_space=pltpu.SMEM),
                      pl.BlockSpec((tile, wcols), lambda i, m: (i, 0))],
            out_specs=pl.BlockSpec(memory_space=pl.ANY),
            scratch_shapes=[pltpu.SemaphoreType.DMA((2,))]),
        out_shape=jax.ShapeDtypeStruct((n_slots, wcols), I32),
        compiler_params=_params(("arbitrary",), disable_bounds_checks=True),
        name="moe_dispatch",
    )(meta, dest, hp)


def _expert_kernel(be_ref, nv_ref, x_ref, wg_ref, wu_ref, wd_ref, o_ref):
    half = D_MODEL // 2

    @pl.when(pl.program_id(0) < nv_ref[0])
    def _():
        w = x_ref[...]
        xa = lax.bitcast_convert_type(w & jnp.int32(-65536), F32).astype(BF16)
        xb = lax.bitcast_convert_type(lax.shift_left(w, 16), F32).astype(BF16)
        hg = (jnp.dot(xa, wg_ref[:half, :], preferred_element_type=F32)
              + jnp.dot(xb, wg_ref[half:, :], preferred_element_type=F32))
        hu = (jnp.dot(xa, wu_ref[:half, :], preferred_element_type=F32)
              + jnp.dot(xb, wu_ref[half:, :], preferred_element_type=F32))
        act = (_silu(hg) * hu).astype(BF16)
        o_ref[...] = jnp.dot(act, wd_ref[...], preferred_element_type=F32)

    @pl.when(pl.program_id(0) >= nv_ref[0])
    def _():
        o_ref[...] = jnp.zeros(o_ref.shape, F32)


def _expert_call(block_e, n_valid, xs, wg, wu, wd, layer):
    n_slots = xs.shape[0]
    tm = MOE_TILE
    nb = n_slots // tm

    def blk(b, be, nv):
        return jnp.minimum(b, nv[0] - 1)

    def wmap(b, be, nv):
        return (layer, be[blk(b, be, nv)], 0, 0)

    return pl.pallas_call(
        _expert_kernel,
        grid_spec=pltpu.PrefetchScalarGridSpec(
            num_scalar_prefetch=2, grid=(nb,),
            in_specs=[pl.BlockSpec((tm, D_MODEL // 2), lambda b, be, nv: (blk(b, be, nv), 0)),
                      pl.BlockSpec((None, None, D_MODEL, D_EXPERT), wmap),
                      pl.BlockSpec((None, None, D_MODEL, D_EXPERT), wmap),
                      pl.BlockSpec((None, None, D_EXPERT, D_MODEL), wmap)],
            out_specs=pl.BlockSpec((tm, D_MODEL), lambda b, be, nv: (b, 0))),
        out_shape=jax.ShapeDtypeStruct((n_slots, D_MODEL), F32),
        compiler_params=_params(("arbitrary",)),
        name="moe_experts",
    )(block_e, n_valid, xs, wg, wu, wd)


def _combine_kernel(dcur_ref, dnext_ref, o_ref, x_ref, gcol_ref, g2_ref, fg_ref, yc_ref, yl_ref, buf, sem,
                    *, tile, nsteps, ctx_tiles):
    i = pl.program_id(0)
    slot = i % 2

    def issue(dref, s):
        def body(t, carry):
            for k in range(2):
                pltpu.make_async_copy(o_ref.at[pl.ds(dref[k, t], 1)], buf.at[s, k, pl.ds(t, 1)], sem.at[s]).start()
            return carry
        lax.fori_loop(0, tile, body, 0, unroll=DMA_UNROLL)

    @pl.when(i == 0)
    def _():
        issue(dcur_ref, 0)

    @pl.when(i + 1 < nsteps)
    def _():
        issue(dnext_ref, 1 - slot)

    for k in range(2):
        pltpu.make_async_copy(o_ref.at[pl.ds(0, tile)], buf.at[slot, k], sem.at[slot]).wait()

    gc = gcol_ref[...]
    ffn = gc[:, 0:1] * buf[slot, 0] + gc[:, 1:2] * buf[slot, 1]
    x2 = x_ref[...] + g2_ref[...] * ffn
    y = x2 * lax.rsqrt(jnp.mean(x2 * x2, axis=-1, keepdims=True) + EPS) * fg_ref[...]

    @pl.when(i < ctx_tiles)
    def _():
        yc_ref[...] = y

    @pl.when(i >= ctx_tiles)
    def _():
        yl_ref[...] = y


def _combine_call(dest, o, x1, gcol, mod3, fg, n_ctx, seq_lat):
    n = x1.shape[0]
    tile = GATHER_TILE
    nsteps = n // tile
    ctx_tiles = n_ctx // tile
    mrow = functools.partial(_mod_row, tile=tile, n_ctx=n_ctx, seq_lat=seq_lat)
    out_specs = [pl.BlockSpec((tile, D_MODEL), lambda i: (jnp.minimum(i, ctx_tiles - 1), 0)),
                 pl.BlockSpec((tile, D_MODEL), lambda i: (jnp.maximum(i - ctx_tiles, 0), 0))]
    out_shape = [jax.ShapeDtypeStruct((n_ctx, D_MODEL), F32), jax.ShapeDtypeStruct((n - n_ctx, D_MODEL), F32)]
    return pl.pallas_call(
        functools.partial(_combine_kernel, tile=tile, nsteps=nsteps, ctx_tiles=ctx_tiles),
        grid=(nsteps,),
        in_specs=[pl.BlockSpec((2, tile), lambda i: (0, i), memory_space=pltpu.SMEM),
                  pl.BlockSpec((2, tile), lambda i: (0, jnp.minimum(i + 1, nsteps - 1)), memory_space=pltpu.SMEM),
                  pl.BlockSpec(memory_space=pl.ANY),
                  pl.BlockSpec((tile, D_MODEL), lambda i: (i, 0)),
                  pl.BlockSpec((tile, LANES), lambda i: (i, 0)),
                  pl.BlockSpec((None, 1, D_MODEL), lambda i: (mrow(i), 0, 5)),
                  pl.BlockSpec((1, D_MODEL), lambda i: (0, 0))],
        out_specs=out_specs,
        out_shape=out_shape,
        scratch_shapes=[pltpu.VMEM((2, 2, tile, D_MODEL), F32), pltpu.SemaphoreType.DMA((2,))],
        compiler_params=_params(("arbitrary",), disable_bounds_checks=True),
        name="moe_combine",
    )(dest, dest, o, x1, gcol, mod3, fg)


def _axial_tables(n, dim):
    t = jnp.arange(n)
    rows = (t // GRID_W).astype(F32)
    cols = (t % GRID_W).astype(F32)
    nf = dim // 4
    inv = jnp.power(ROPE_BASE, -jnp.arange(nf, dtype=F32) / nf)
    ang = jnp.concatenate([rows[:, None] * inv, cols[:, None] * inv], axis=-1)
    return jnp.cos(ang), jnp.sin(ang)


def _rope_lane_tables(n, dim):
    cos, sin = _axial_tables(n, dim)
    c = jnp.concatenate([cos, cos], axis=-1)
    s = jnp.concatenate([-sin, sin], axis=-1)
    pad = LANES - dim
    if pad:
        c = jnp.pad(c, ((0, 0), (0, pad)))
        s = jnp.pad(s, ((0, 0), (0, pad)))
    return c, s


def _identity_rows(n, dim):
    c = jnp.pad(jnp.ones((n, dim), F32), ((0, 0), (0, LANES - dim)))
    return c, jnp.zeros((n, LANES), F32)


def _swap_halves_cols(w):
    half = w.shape[-1] // 2
    return jnp.concatenate([w[..., half:], w[..., :half]], axis=-1)


def _moe_plan(eid, rank, counts, n_slots):
    tm = MOE_TILE
    counts = counts.astype(I32)
    padded = (counts + tm - 1) // tm * tm
    pend = jnp.cumsum(padded)
    pstart = pend - padded
    start_of = jnp.zeros(eid.shape, I32)
    for e in range(N_EXPERTS):
        start_of = jnp.where(eid == e, pstart[e], start_of)
    dest = (start_of + rank).astype(I32)
    n_valid = (pend[-1] // tm).astype(I32)
    blk_start = jnp.arange(n_slots // tm, dtype=I32) * tm
    block_e = jnp.minimum(jnp.sum((blk_start[:, None] >= pend[None, :]).astype(I32), axis=1), N_EXPERTS - 1)
    meta = jnp.stack([pstart + counts, padded - counts, jnp.broadcast_to(n_valid, counts.shape)]).astype(I32)
    return dest, block_e.astype(I32), n_valid.reshape(1), meta


def kernel(x_prompt, x_sample, c, cache_ckv, cache_krope, state_ret, c_ctx, w_ada, b_ada, norm_attn, norm_ffn,
           w_in, ret_decay_logit, q_norm, kv_norm, w_uq, w_ukv, w_out, w_router, b_router, w_exp_gate, w_exp_up,
           w_exp_down, final_norm):
    batch, seq, d = x_prompt.shape
    dbatch, dseq, _ = x_sample.shape
    depth = w_ada.shape[0]
    past = cache_ckv.shape[2]
    n_ctx, n_lat = batch * seq, dbatch * dseq
    n = n_ctx + n_lat
    assert d == D_MODEL and dbatch + 1 <= 8
    assert n_ctx % dseq == 0 and dseq % ROW_TILE == 0 and n_ctx % ROW_TILE == 0 and seq % RET_CHUNK == 0
    n_slots = (2 * n // MOE_TILE + N_EXPERTS) * MOE_TILE

    cond = jnp.concatenate([c_ctx[None, :], c, jnp.zeros((8 - 1 - dbatch, d), F32)], axis=0)
    mod = _ada_call(cond, w_ada, b_ada)

    cr, sr = _rope_lane_tables(dseq, D_RET)
    cm, sm = _rope_lane_tables(dseq, D_ROPE)
    one_q, zero_q = _identity_rows(ROW_TILE, D_ROPE)
    cos_q, sin_q = jnp.concatenate([one_q, cm], axis=0), jnp.concatenate([zero_q, sm], axis=0)
    one_p, zero_p = _identity_rows(past, D_ROPE)
    cos_k, sin_k = jnp.concatenate([one_p, cm], axis=0), jnp.concatenate([zero_p, sm], axis=0)
    one_c, zero_c = _identity_rows(seq, D_ROPE)
    eye = jnp.eye(LANES, dtype=BF16)
    cache_krope_p = jnp.pad(cache_krope, ((0, 0), (0, 0), (0, 0), (0, LANES - D_ROPE)))

    wr = jnp.pad(w_router, ((0, 0), (0, LANES - N_EXPERTS))).astype(BF16)
    br = b_router.reshape(N_EXPERTS, 1).astype(F32)
    lg_all = jax.nn.log_sigmoid(ret_decay_logit.astype(F32))
    wg, wu, wd = w_exp_gate.astype(BF16), w_exp_up.astype(BF16), w_exp_down.astype(BF16)

    ckv_layers, krope_layers, state_layers = [], [], []
    pending = None
    for l in range(depth):
        mod3 = mod[l].reshape(8, 1, 6 * d)
        wl = w_in[l]
        cuts = 4 * RET_W + Q_RANK + KV_RANK
        w_in_p = jnp.concatenate(
            [wl, _swap_halves_cols(wl[:, cuts:]), jnp.zeros((d, IN_W - wl.shape[1] - D_ROPE), F32)],
            axis=1).astype(BF16)
        wq = w_uq[l].reshape(Q_RANK, H_MLA, D_NOPE + D_ROPE)
        w_uq_p = jnp.concatenate([wq, _swap_halves_cols(wq[..., D_NOPE:])], axis=-1)
        w_uq_p = w_uq_p.reshape(Q_RANK, H_MLA * Q_HEAD_W).astype(BF16)
        wkv = w_ukv[l].reshape(KV_RANK, H_MLA, D_NOPE + D_V)
        wkt = wkv[..., :D_NOPE].transpose(1, 2, 0).astype(BF16)
        wv = wkv[..., D_NOPE:].transpose(1, 0, 2).astype(BF16)
        wo = w_out[l].astype(BF16)

        if pending is None:
            x, qkvg, cqn, ckv, kr = _inproj_call(x_prompt.reshape(n_ctx, d), x_sample.reshape(n_lat, d), mod3,
                                                 norm_attn[l][None, :], w_in_p, q_norm[l][None, :],
                                                 kv_norm[l][None, :], dseq)
        else:
            x, qkvg, cqn, ckv, kr = _combine_inproj_call(*pending, mod3, norm_attn[l][None, :], w_in_p,
                                                         q_norm[l][None, :], kv_norm[l][None, :], n_ctx, dseq)
        ckv_layers.append(ckv[:n_ctx].reshape(batch, seq, KV_RANK))
        krope_layers.append(kr[:n_ctx, :D_ROPE].reshape(batch, seq, D_ROPE))

        ret_c, s_ctx = _retention_call(lg_all[l], qkvg, batch, seq, 0, H_RET, seq // RET_CHUNK, emit_state=True)
        (ret_l,) = _retention_call(lg_all[l], qkvg, dbatch, dseq, n_ctx // dseq, 2, 8, s0=state_ret[:, l],
                                   tables=(cr, sr))
        state_layers.append(s_ctx)

        q = _qproj_call(cqn, w_uq_p, cos_q, sin_q, n_ctx, dseq)
        kt_c, v_c = _kvproj_call(ckv, kr, batch, seq, 0, one_c, zero_c, wkt, wv, eye, H_MLA)
        kt_l, v_l = _kvproj_call(ckv, kr, dbatch, dseq, n_ctx // dseq, cos_k, sin_k, wkt, wv, eye, 2,
                                 cache=(cache_ckv, cache_krope_p, l))
        mla_c = _attn_call(q, kt_c, v_c, seq, 0, seq, H_MLA)
        mla_l = _attn_call(q, kt_l, v_l, dseq, n_ctx, Q_TILE, 4)

        x1, hp, eid, rank, gcol, cnt = _outproj_call(x, (ret_c, ret_l), (mla_c, mla_l), wo, mod3,
                                                     norm_ffn[l][None, :], wr, br, n_ctx, dseq)

        dest, block_e, n_valid, meta = _moe_plan(eid, rank, cnt[:, 0], n_slots)
        xs = _dispatch_call(meta, dest, hp, n_slots)
        o = _expert_call(block_e, n_valid, xs, wg, wu, wd, l)
        pending = (dest, o, x1, gcol, mod3)

    y_ctx, y_lat = _combine_call(*pending, final_norm[None, :], n_ctx, dseq)
    return (y_ctx.reshape(batch, seq, d), y_lat.reshape(dbatch, dseq, d), jnp.stack(ckv_layers, axis=1),
            jnp.stack(krope_layers, axis=1), jnp.stack(state_layers, axis=1))
```

```python
import functools
import math

import jax
import jax.numpy as jnp
from jax import lax
from jax.experimental import pallas as pl
from jax.experimental.pallas import tpu as pltpu

F32 = jnp.float32
BF16 = jnp.bfloat16
I32 = jnp.int32

D_MODEL = 2048
GRID_W = 64
H_RET = 8
D_RET = 128
RET_W = H_RET * D_RET
RET_CHUNK = 128
H_MLA = 8
D_NOPE = 128
D_ROPE = 64
D_V = 128
Q_RANK = 512
KV_RANK = 256
N_EXPERTS = 16
N_GROUPS = 4
EXPERTS_PER_GROUP = N_EXPERTS // N_GROUPS
D_EXPERT = 1024
ROPE_BASE = 10000.0
EPS = 1e-6

LANES = 128
Q_HEAD_W = 2 * LANES
LAT_W = 1024
IN_W = 4 * RET_W + LAT_W
ROW_TILE = 512
OUT_TILE = 256
Q_TILE = 256
MOE_TILE = 256
GATHER_TILE = 256
DMA_UNROLL = 8
VMEM_LIMIT = 56 * 1024 * 1024


def _params(sem, **kw):
    return pltpu.CompilerParams(dimension_semantics=sem, vmem_limit_bytes=VMEM_LIMIT, **kw)


def _silu(x):
    return x * jax.nn.sigmoid(x)


def _mod_row(i, tile, n_ctx, seq_lat):
    r0 = i * tile
    return jnp.where(r0 < n_ctx, 0, 1 + (r0 - n_ctx) // seq_lat)


def _pos_block(i, tile, n_ctx, seq_lat):
    r0 = i * tile
    return jnp.where(r0 < n_ctx, 0, 1 + ((r0 - n_ctx) % seq_lat) // tile)


def _ada_kernel(c_ref, w_ref, b_ref, o_ref):
    a = _silu(c_ref[...]).astype(BF16)
    o_ref[...] = jnp.dot(a, w_ref[...].astype(BF16), preferred_element_type=F32) + b_ref[...]


def _ada_call(cond, w_ada, b_ada):
    depth, d, n6 = w_ada.shape
    tn = 1536
    return pl.pallas_call(
        _ada_kernel,
        grid=(depth, n6 // tn),
        in_specs=[pl.BlockSpec((8, d), lambda l, j: (0, 0)),
                  pl.BlockSpec((None, d, tn), lambda l, j: (l, 0, j)),
                  pl.BlockSpec((None, 1, tn), lambda l, j: (l, 0, j))],
        out_specs=pl.BlockSpec((None, 8, tn), lambda l, j: (l, 0, j)),
        out_shape=jax.ShapeDtypeStruct((depth, 8, n6), F32),
        compiler_params=_params(("arbitrary", "arbitrary")),
        name="adaln",
    )(cond, w_ada, b_ada.reshape(depth, 1, n6))


def _inproj_kernel(xc_ref, xl_ref, sh_ref, sc_ref, ng_ref, w_ref, qg_ref, kvg_ref, x_ref, qkvg_ref, cq_ref,
                   ckv_ref, kr_ref, *, ctx_tiles):
    x = jnp.where(pl.program_id(0) < ctx_tiles, xc_ref[...], xl_ref[...])
    x_ref[...] = x
    _project_rows(x, sh_ref, sc_ref, ng_ref, w_ref, qg_ref, kvg_ref, qkvg_ref, cq_ref, ckv_ref, kr_ref,
                  slice(None))


def _inproj_call(x_ctx, x_lat, mod3, ng, w_in_p, qg, kvg, seq_lat):
    n_ctx = x_ctx.shape[0]
    n = n_ctx + x_lat.shape[0]
    tm = GATHER_TILE
    ctx_tiles = n_ctx // tm
    mrow = functools.partial(_mod_row, tile=tm, n_ctx=n_ctx, seq_lat=seq_lat)
    return pl.pallas_call(
        functools.partial(_inproj_kernel, ctx_tiles=ctx_tiles),
        grid=(n // tm,),
        in_specs=[pl.BlockSpec((tm, D_MODEL), lambda i: (jnp.minimum(i, ctx_tiles - 1), 0)),
                  pl.BlockSpec((tm, D_MODEL), lambda i: (jnp.maximum(i - ctx_tiles, 0), 0)),
                  pl.BlockSpec((None, 1, D_MODEL), lambda i: (mrow(i), 0, 0)),
                  pl.BlockSpec((None, 1, D_MODEL), lambda i: (mrow(i), 0, 1)),
                  pl.BlockSpec((1, D_MODEL), lambda i: (0, 0)),
                  pl.BlockSpec((D_MODEL, IN_W), lambda i: (0, 0), pipeline_mode=pl.Buffered(1)),
                  pl.BlockSpec((1, Q_RANK), lambda i: (0, 0)),
                  pl.BlockSpec((1, KV_RANK), lambda i: (0, 0))],
        out_specs=[pl.BlockSpec((tm, D_MODEL), lambda i: (i, 0)),
                   pl.BlockSpec((tm, 4 * RET_W), lambda i: (i, 0)),
                   pl.BlockSpec((tm, Q_RANK), lambda i: (i, 0)),
                   pl.BlockSpec((tm, KV_RANK), lambda i: (i, 0)),
                   pl.BlockSpec((tm, LANES), lambda i: (i, 0))],
        out_shape=[jax.ShapeDtypeStruct((n, D_MODEL), F32),
                   jax.ShapeDtypeStruct((n, 4 * RET_W), BF16),
                   jax.ShapeDtypeStruct((n, Q_RANK), BF16),
                   jax.ShapeDtypeStruct((n, KV_RANK), F32),
                   jax.ShapeDtypeStruct((n, LANES), F32)],
        compiler_params=_params(("arbitrary",)),
        name="inproj",
    )(x_ctx, x_lat, mod3, mod3, ng, w_in_p, qg, kvg)


def _project_rows(x, sh_ref, sc_ref, ng_ref, w_ref, qg_ref, kvg_ref, qkvg_ref, cq_ref, ckv_ref, kr_ref, rs,
                  after_chunk=lambda j: None):
    y = x * lax.rsqrt(jnp.mean(x * x, axis=-1, keepdims=True) + EPS) * ng_ref[...]
    h = (y * (1.0 + sc_ref[...]) + sh_ref[...]).astype(BF16)
    for j in range(4 * RET_W // LAT_W):
        cs = slice(j * LAT_W, (j + 1) * LAT_W)
        qkvg_ref[rs, cs] = jnp.dot(h, w_ref[:, cs], preferred_element_type=F32).astype(BF16)
        after_chunk(j)
    acc = jnp.dot(h, w_ref[:, 4 * RET_W:], preferred_element_type=F32)
    cq = acc[:, :Q_RANK]
    cq = cq * lax.rsqrt(jnp.mean(cq * cq, axis=-1, keepdims=True) + EPS) * qg_ref[...]
    cq_ref[rs, :] = cq.astype(BF16)
    ckv = acc[:, Q_RANK:Q_RANK + KV_RANK]
    ckv_ref[rs, :] = ckv * lax.rsqrt(jnp.mean(ckv * ckv, axis=-1, keepdims=True) + EPS) * kvg_ref[...]
    kr_ref[rs, :] = acc[:, Q_RANK + KV_RANK:Q_RANK + KV_RANK + LANES]


def _combine_inproj_kernel(dcur_ref, dnext_ref, o_ref, x1_ref, gcol_ref, g2_ref, sh_ref, sc_ref, ng_ref, w_ref,
                           qg_ref, kvg_ref, x2_ref, qkvg_ref, cq_ref, ckv_ref, kr_ref, buf, sem, *, tile, nsteps):
    i = pl.program_id(0)
    slot = i % 2

    def row_copy(dref, s, k, t):
        return pltpu.make_async_copy(o_ref.at[pl.ds(dref[k, t], 1)], buf.at[s, k, pl.ds(t, 1)], sem.at[s])

    @pl.when(i == 0)
    def _():
        def body(t, carry):
            for k in range(2):
                row_copy(dcur_ref, 0, k, t).start()
            return carry
        lax.fori_loop(0, tile, body, 0, unroll=DMA_UNROLL)

    for k in range(2):
        pltpu.make_async_copy(o_ref.at[pl.ds(0, tile)], buf.at[slot, k], sem.at[slot]).wait()

    gc = gcol_ref[...]
    x2 = x1_ref[...] + g2_ref[...] * (gc[:, 0:1] * buf[slot, 0] + gc[:, 1:2] * buf[slot, 1])
    x2_ref[...] = x2

    nchunks = 4 * RET_W // LAT_W
    share = tile // nchunks

    def request_rows(j):
        for t in range(j * share, (j + 1) * share):
            for k in range(2):
                row_copy(dnext_ref, 1 - slot, k, t).start()

    _project_rows(x2, sh_ref, sc_ref, ng_ref, w_ref, qg_ref, kvg_ref, qkvg_ref, cq_ref, ckv_ref, kr_ref,
                  slice(None), after_chunk=request_rows)

    @pl.when(i == nsteps - 1)
    def _():
        for k in range(2):
            pltpu.make_async_copy(o_ref.at[pl.ds(0, tile)], buf.at[1 - slot, k], sem.at[1 - slot]).wait()


def _combine_inproj_call(dest, o, x1, gcol, mod_prev, mod3, ng, w_in_p, qg, kvg, n_ctx, seq_lat):
    n = x1.shape[0]
    tile = GATHER_TILE
    nsteps = n // tile
    mrow = functools.partial(_mod_row, tile=tile, n_ctx=n_ctx, seq_lat=seq_lat)

    def rows(width):
        return pl.BlockSpec((tile, width), lambda i: (i, 0))

    def const(shape):
        return pl.BlockSpec(shape, lambda i: (0, 0))

    return pl.pallas_call(
        functools.partial(_combine_inproj_kernel, tile=tile, nsteps=nsteps),
        grid=(nsteps,),
        in_specs=[pl.BlockSpec((2, tile), lambda i: (0, i), memory_space=pltpu.SMEM),
                  pl.BlockSpec((2, tile), lambda i: (0, jnp.minimum(i + 1, nsteps - 1)), memory_space=pltpu.SMEM),
                  pl.BlockSpec(memory_space=pl.ANY),
                  rows(D_MODEL), rows(LANES),
                  pl.BlockSpec((None, 1, D_MODEL), lambda i: (mrow(i), 0, 5)),
                  pl.BlockSpec((None, 1, D_MODEL), lambda i: (mrow(i), 0, 0)),
                  pl.BlockSpec((None, 1, D_MODEL), lambda i: (mrow(i), 0, 1)),
                  const((1, D_MODEL)),
                  pl.BlockSpec((D_MODEL, IN_W), lambda i: (0, 0), pipeline_mode=pl.Buffered(1)),
                  const((1, Q_RANK)), const((1, KV_RANK))],
        out_specs=[rows(D_MODEL), rows(4 * RET_W), rows(Q_RANK), rows(KV_RANK), rows(LANES)],
        out_shape=[jax.ShapeDtypeStruct((n, D_MODEL), F32),
                   jax.ShapeDtypeStruct((n, 4 * RET_W), BF16),
                   jax.ShapeDtypeStruct((n, Q_RANK), BF16),
                   jax.ShapeDtypeStruct((n, KV_RANK), F32),
                   jax.ShapeDtypeStruct((n, LANES), F32)],
        scratch_shapes=[pltpu.VMEM((2, 2, tile, D_MODEL), F32), pltpu.SemaphoreType.DMA((2,))],
        compiler_params=_params(("arbitrary",), disable_bounds_checks=True),
        name="combine_inproj",
    )(dest, dest, o, x1, gcol, mod_prev, mod3, mod3, ng, w_in_p, qg, kvg)


def _retention_kernel(lg_ref, *refs, n, hb, unroll, rope, has_s0, emit_state):
    it = iter(refs)
    q_ref, k_ref, v_ref, g_ref = next(it), next(it), next(it), next(it)
    s0_ref = next(it) if has_s0 else None
    cos_ref, sin_ref = (next(it), next(it)) if rope else (None, None)
    o_ref = next(it)
    sfin_ref = next(it) if emit_state else None
    kvf_scr, kvb_scr, kt_scr, dm_scr = next(it), next(it), next(it), next(it)

    C = RET_CHUNK
    nc = n // C
    hblk = pl.program_id(1)
    diff = (lax.broadcasted_iota(I32, (C, C), 0) - lax.broadcasted_iota(I32, (C, C), 1)).astype(F32)
    col = lax.broadcasted_iota(I32, (C, 1), 0).astype(F32)
    lane = lax.broadcasted_iota(I32, (1, C), 1).astype(F32)
    k_scale = D_RET ** -0.5

    xi_f, xi_b, zeta_f, zeta_b, g_f, g_b = [], [], [], [], [], []
    for hh in range(hb):
        lgf = lg_ref[0, hblk * hb + hh]
        lgb = lg_ref[1, hblk * hb + hh]
        dm_scr[hh] = (jnp.where(diff >= 0, jnp.exp(jnp.maximum(diff, 0.0) * lgf), 0.0)
                      + jnp.where(diff <= 0, jnp.exp(jnp.maximum(-diff, 0.0) * lgb), 0.0))
        xi_f.append(jnp.exp((col + 1.0) * lgf))
        xi_b.append(jnp.exp((C - col) * lgb))
        zeta_f.append(jnp.exp((C - 1.0 - lane) * lgf))
        zeta_b.append(jnp.exp(lane * lgb))
        g_f.append(jnp.exp(jnp.full((1, C), float(C), F32) * lgf))
        g_b.append(jnp.exp(jnp.full((1, C), float(C), F32) * lgb))

    def rows(c):
        return pl.ds(pl.multiple_of(c * C, C), C)

    def cols(hh):
        return slice(hh * C, (hh + 1) * C)

    def rot(x, c):
        if not rope:
            return x
        return x * cos_ref[rows(c), :] + pltpu.roll(x, C // 2, 1) * sin_ref[rows(c), :]

    def phase_a(c, carry):
        for hh in range(hb):
            kt = rot(k_ref[rows(c), cols(hh)].astype(F32) * k_scale, c).T
            kt_scr[hh, c] = kt.astype(BF16)
            vc = v_ref[rows(c), cols(hh)]
            kvf_scr[hh, c] = jnp.dot((kt * zeta_f[hh]).astype(BF16), vc, preferred_element_type=F32)
            kvb_scr[hh, c] = jnp.dot((kt * zeta_b[hh]).astype(BF16), vc, preferred_element_type=F32)
        return carry

    lax.fori_loop(0, nc, phase_a, 0, unroll=unroll)

    for hh in range(hb):
        def scan(i, carry, hh=hh):
            sf, sb = carry
            cb = nc - 1 - i
            upd_f = kvf_scr[hh, i]
            kvf_scr[hh, i] = sf
            upd_b = kvb_scr[hh, cb]
            kvb_scr[hh, cb] = sb
            return g_f[hh] * sf + upd_f, g_b[hh] * sb + upd_b

        if has_s0:
            init = (s0_ref[0, hh], s0_ref[1, hh])
        else:
            init = (jnp.zeros((C, C), F32), jnp.zeros((C, C), F32))
        s_f, s_b = lax.fori_loop(0, nc, scan, init)
        if emit_state:
            sfin_ref[0, hh] = s_f
            sfin_ref[1, hh] = s_b

    def phase_c(c, carry):
        for hh in range(hb):
            qc = rot(q_ref[rows(c), cols(hh)].astype(F32), c)
            inner = jnp.dot(qc.astype(BF16), kt_scr[hh, c], preferred_element_type=F32) * dm_scr[hh]
            o = (jnp.dot(inner.astype(BF16), v_ref[rows(c), cols(hh)], preferred_element_type=F32)
                 + jnp.dot((qc * xi_f[hh]).astype(BF16), kvf_scr[hh, c].astype(BF16), preferred_element_type=F32)
                 + jnp.dot((qc * xi_b[hh]).astype(BF16), kvb_scr[hh, c].astype(BF16), preferred_element_type=F32))
            dev = o - jnp.mean(o, axis=-1, keepdims=True)
            on = dev * lax.rsqrt(jnp.mean(dev * dev, axis=-1, keepdims=True) + EPS)
            gate = g_ref[rows(c), cols(hh)].astype(F32)
            o_ref[rows(c), cols(hh)] = (on * _silu(gate)).astype(BF16)
        return carry

    lax.fori_loop(0, nc, phase_c, 0, unroll=unroll)


def _retention_call(lg2, qkvg, batch, n, row_blk0, hb, unroll, *, s0=None, tables=None, emit_state=False):
    hblks = H_RET // hb
    w = hb * D_RET
    nc = n // RET_CHUNK

    def qspec(part):
        return pl.BlockSpec((n, w), lambda b, h, lg: (row_blk0 + b, part * hblks + h))

    in_specs = [qspec(0), qspec(1), qspec(2), qspec(3)]
    args = [qkvg, qkvg, qkvg, qkvg]
    if s0 is not None:
        in_specs.append(pl.BlockSpec((None, 2, hb, D_RET, D_RET), lambda b, h, lg: (b, 0, h, 0, 0)))
        args.append(s0)
    if tables is not None:
        in_specs += [pl.BlockSpec((n, D_RET), lambda b, h, lg: (0, 0))] * 2
        args += list(tables)
    out_specs = [pl.BlockSpec((n, w), lambda b, h, lg: (b, h))]
    out_shape = [jax.ShapeDtypeStruct((batch * n, RET_W), BF16)]
    if emit_state:
        out_specs.append(pl.BlockSpec((None, 2, hb, D_RET, D_RET), lambda b, h, lg: (b, 0, h, 0, 0)))
        out_shape.append(jax.ShapeDtypeStruct((batch, 2, H_RET, D_RET, D_RET), F32))
    kern = functools.partial(_retention_kernel, n=n, hb=hb, unroll=unroll, rope=tables is not None,
                             has_s0=s0 is not None, emit_state=emit_state)
    return pl.pallas_call(
        kern,
        grid_spec=pltpu.PrefetchScalarGridSpec(
            num_scalar_prefetch=1, grid=(batch, hblks), in_specs=in_specs, out_specs=out_specs,
            scratch_shapes=[pltpu.VMEM((hb, nc, D_RET, D_RET), F32), pltpu.VMEM((hb, nc, D_RET, D_RET), F32),
                            pltpu.VMEM((hb, nc, D_RET, D_RET), BF16), pltpu.VMEM((hb, D_RET, D_RET), F32)]),
        out_shape=out_shape,
        compiler_params=_params(("arbitrary", "arbitrary")),
        name="retention",
    )(lg2, *args)


def _qproj_kernel(cq_ref, w_ref, cos_ref, sin_ref, q_ref):
    scale = (D_NOPE + D_ROPE) ** -0.5 * math.log2(math.e)
    acc = jnp.dot(cq_ref[...], w_ref[...], preferred_element_type=F32)
    for h in range(H_MLA):
        base = h * Q_HEAD_W
        q_ref[:, base:base + LANES] = (acc[:, base:base + LANES] * scale).astype(BF16)
        tail = acc[:, base + LANES:base + Q_HEAD_W]
        rot = tail * cos_ref[...] + pltpu.roll(tail, D_ROPE, 1) * sin_ref[...]
        q_ref[:, base + LANES:base + Q_HEAD_W] = (rot * scale).astype(BF16)


def _qproj_call(cqn, w_uq_p, cos_q, sin_q, n_ctx, seq_lat):
    n = cqn.shape[0]
    tm = ROW_TILE
    pblk = functools.partial(_pos_block, tile=tm, n_ctx=n_ctx, seq_lat=seq_lat)
    return pl.pallas_call(
        _qproj_kernel,
        grid=(n // tm,),
        in_specs=[pl.BlockSpec((tm, Q_RANK), lambda i: (i, 0)),
                  pl.BlockSpec((Q_RANK, H_MLA * Q_HEAD_W), lambda i: (0, 0)),
                  pl.BlockSpec((tm, LANES), lambda i: (pblk(i), 0)),
                  pl.BlockSpec((tm, LANES), lambda i: (pblk(i), 0))],
        out_specs=pl.BlockSpec((tm, H_MLA * Q_HEAD_W), lambda i: (i, 0)),
        out_shape=jax.ShapeDtypeStruct((n, H_MLA * Q_HEAD_W), BF16),
        compiler_params=_params(("arbitrary",)),
        name="qproj",
    )(cqn, w_uq_p, cos_q, sin_q)


def _kvproj_kernel(*refs, hb, past):
    if past:
        cckv_ref, ckr_ref, ckv_ref, kr_ref, cos_ref, sin_ref, wkt_ref, wv_ref, eye_ref, kt_ref, v_ref = refs
        parts = [(0, past, cckv_ref, ckr_ref), (past, kt_ref.shape[-1], ckv_ref, kr_ref)]
    else:
        ckv_ref, kr_ref, cos_ref, sin_ref, wkt_ref, wv_ref, eye_ref, kt_ref, v_ref = refs
        parts = [(0, kt_ref.shape[-1], ckv_ref, kr_ref)]
    nt = (((1,), (1,)), ((), ()))
    for lo, hi, c_ref, r_ref in parts:
        keys = c_ref[...].astype(BF16)
        kr = r_ref[...]
        rot = (kr * cos_ref[lo:hi, :] + pltpu.roll(kr, D_ROPE, 1) * sin_ref[lo:hi, :]).astype(BF16)
        rope_t = lax.dot_general(eye_ref[...], rot, nt, preferred_element_type=F32).astype(BF16)
        for hh in range(hb):
            kt_ref[hh, :D_NOPE, lo:hi] = lax.dot_general(wkt_ref[hh], keys, nt,
                                                         preferred_element_type=F32).astype(BF16)
            kt_ref[hh, D_NOPE:, lo:hi] = rope_t
            v_ref[hh, lo:hi, :] = jnp.dot(keys, wv_ref[hh], preferred_element_type=F32).astype(BF16)


def _kvproj_call(ckv, kr, batch, n, row_blk0, cos_k, sin_k, wkt, wv, eye, hb, cache=None):
    hblks = H_MLA // hb
    past = 0 if cache is None else cache[0].shape[2]
    nk = past + n
    in_specs, args = [], []
    if cache is not None:
        layer = cache[2]
        in_specs += [pl.BlockSpec((None, None, past, KV_RANK), lambda b, h: (b, layer, 0, 0)),
                     pl.BlockSpec((None, None, past, LANES), lambda b, h: (b, layer, 0, 0))]
        args += [cache[0], cache[1]]
    in_specs += [pl.BlockSpec((n, KV_RANK), lambda b, h: (row_blk0 + b, 0)),
                 pl.BlockSpec((n, LANES), lambda b, h: (row_blk0 + b, 0)),
                 pl.BlockSpec((nk, LANES), lambda b, h: (0, 0)),
                 pl.BlockSpec((nk, LANES), lambda b, h: (0, 0)),
                 pl.BlockSpec((hb, D_NOPE, KV_RANK), lambda b, h: (h, 0, 0)),
                 pl.BlockSpec((hb, KV_RANK, D_V), lambda b, h: (h, 0, 0)),
                 pl.BlockSpec((LANES, LANES), lambda b, h: (0, 0))]
    args += [ckv, kr, cos_k, sin_k, wkt, wv, eye]
    return pl.pallas_call(
        functools.partial(_kvproj_kernel, hb=hb, past=past),
        grid=(batch, hblks),
        in_specs=in_specs,
        out_specs=[pl.BlockSpec((None, hb, Q_HEAD_W, nk), lambda b, h: (b, h, 0, 0)),
                   pl.BlockSpec((None, hb, nk, D_V), lambda b, h: (b, h, 0, 0))],
        out_shape=[jax.ShapeDtypeStruct((batch, H_MLA, Q_HEAD_W, nk), BF16),
                   jax.ShapeDtypeStruct((batch, H_MLA, nk, D_V), BF16)],
        compiler_params=_params(("arbitrary", "arbitrary")),
        name="kvproj",
    )(*args)


def _attn_kernel(q_ref, kt_ref, v_ref, o_ref, *, hb):
    for hh in range(hb):
        q = q_ref[:, hh * Q_HEAD_W:(hh + 1) * Q_HEAD_W]
        s = jnp.dot(q, kt_ref[hh], preferred_element_type=F32)
        p = jnp.exp2(s - jnp.max(s, axis=-1, keepdims=True))
        den = jnp.sum(p, axis=-1, keepdims=True)
        o = jnp.dot(p.astype(BF16), v_ref[hh], preferred_element_type=F32)
        o_ref[:, hh * D_V:(hh + 1) * D_V] = (o / den).astype(BF16)


def _attn_call(q, kt, v, n, row0, tq, hb):
    batch, _, _, nk = kt.shape
    hblks = H_MLA // hb
    nq = n // tq
    blk0 = row0 // tq
    return pl.pallas_call(
        functools.partial(_attn_kernel, hb=hb),
        grid=(batch, hblks, nq),
        in_specs=[pl.BlockSpec((tq, hb * Q_HEAD_W), lambda b, h, i: (blk0 + b * nq + i, h)),
                  pl.BlockSpec((None, hb, Q_HEAD_W, nk), lambda b, h, i: (b, h, 0, 0)),
                  pl.BlockSpec((None, hb, nk, D_V), lambda b, h, i: (b, h, 0, 0))],
        out_specs=pl.BlockSpec((tq, hb * D_V), lambda b, h, i: (b * nq + i, h)),
        out_shape=jax.ShapeDtypeStruct((batch * n, H_MLA * D_V), BF16),
        compiler_params=_params(("arbitrary", "arbitrary", "arbitrary")),
        name="mla_attention",
    )(q, kt, v)


def _route_rows(s_rows, sel_rows):
    epg = EXPERTS_PER_GROUP
    best_g = best_v = None
    for g in range(N_GROUPS):
        v = sel_rows[g * epg:(g + 1) * epg]
        top2 = None
        for a in range(epg):
            for b in range(a + 1, epg):
                pair = v[a] + v[b]
                top2 = pair if top2 is None else jnp.maximum(top2, pair)
        if g == 0:
            best_v, best_g = top2, jnp.zeros(top2.shape, I32)
        else:
            upd = top2 > best_v
            best_v = jnp.where(upd, top2, best_v)
            best_g = jnp.where(upd, g, best_g)

    def pick(rows_, i):
        out = rows_[i]
        for g in range(1, N_GROUPS):
            out = jnp.where(best_g == g, rows_[g * epg + i], out)
        return out

    w = [pick(sel_rows, i) for i in range(epg)]
    sv = [pick(s_rows, i) for i in range(epg)]
    m0, i0, s0 = w[0], jnp.zeros(w[0].shape, I32), sv[0]
    for i in range(1, epg):
        upd = w[i] > m0
        m0, i0, s0 = jnp.where(upd, w[i], m0), jnp.where(upd, i, i0), jnp.where(upd, sv[i], s0)
    m1 = jnp.full(w[0].shape, -jnp.inf, F32)
    i1, s1 = jnp.zeros(w[0].shape, I32), sv[0]
    for i in range(epg):
        upd = (i0 != i) & (w[i] > m1)
        m1, i1, s1 = jnp.where(upd, w[i], m1), jnp.where(upd, i, i1), jnp.where(upd, sv[i], s1)
    tot = s0 + s1
    return best_g * epg + i0, best_g * epg + i1, s0 / tot, s1 / tot


def _outproj_kernel(x_ref, mrc_ref, mrl_ref, mmc_ref, mml_ref, wo_ref, g1_ref, sc_ref, sh_ref, ng_ref, wr_ref,
                    br_ref, x1_ref, hp_ref, eid_ref, rank_ref, gcol_ref, cnt_ref, *, ctx_tiles):
    half = D_MODEL // 2

    @pl.when(pl.program_id(0) == 0)
    def _():
        cnt_ref[...] = jnp.zeros(cnt_ref.shape, F32)

    is_ctx = pl.program_id(0) < ctx_tiles
    mr = jnp.where(is_ctx, mrc_ref[...], mrl_ref[...])
    mm = jnp.where(is_ctx, mmc_ref[...], mml_ref[...])
    mix = (jnp.dot(mr, wo_ref[:RET_W, :], preferred_element_type=F32)
           + jnp.dot(mm, wo_ref[RET_W:, :], preferred_element_type=F32))
    x1 = x_ref[...] + g1_ref[...] * mix
    x1_ref[...] = x1
    y = x1 * lax.rsqrt(jnp.mean(x1 * x1, axis=-1, keepdims=True) + EPS) * ng_ref[...]
    h2 = (y * (1.0 + sc_ref[...]) + sh_ref[...]).astype(BF16)
    hi = lax.bitcast_convert_type(h2[:, :half].astype(F32), I32)
    lo = lax.bitcast_convert_type(h2[:, half:].astype(F32), I32)
    hp_ref[...] = (hi & jnp.int32(-65536)) | lax.shift_right_logical(lo, 16)

    logits = jnp.dot(h2, wr_ref[...], preferred_element_type=F32)
    lt = logits.T
    s_rows = [jax.nn.sigmoid(lt[e:e + 1, :]) for e in range(N_EXPERTS)]
    sel_rows = [s_rows[e] + br_ref[e:e + 1, :] for e in range(N_EXPERTS)]
    e0, e1, g0, g1 = _route_rows(s_rows, sel_rows)
    eid_ref[0:1, :] = e0
    eid_ref[1:2, :] = e1
    t = g0.shape[1]
    erow = lax.broadcasted_iota(I32, (N_EXPERTS, t), 0)
    hit0 = erow == e0
    hit1 = erow == e1
    onehot = jnp.where(hit0 | hit1, 1.0, 0.0)
    before = lax.broadcasted_iota(I32, (t, t), 0) < lax.broadcasted_iota(I32, (t, t), 1)
    prefix = jnp.dot(onehot.astype(BF16), jnp.where(before, 1.0, 0.0).astype(BF16), preferred_element_type=F32)
    seen = prefix + cnt_ref[:, 0:1]
    rank_ref[0:1, :] = jnp.sum(jnp.where(hit0, seen, 0.0), axis=0, keepdims=True).astype(I32)
    rank_ref[1:2, :] = jnp.sum(jnp.where(hit1, seen, 0.0), axis=0, keepdims=True).astype(I32)
    cnt_ref[...] = cnt_ref[...] + jnp.sum(onehot, axis=1, keepdims=True)
    row = lax.broadcasted_iota(I32, (LANES, t), 0)
    gcol_ref[...] = jnp.where(row == 0, g0, jnp.where(row == 1, g1, 0.0)).T


def _outproj_call(x, mix_ret, mix_mla, wo, mod3, ng, wr, br, n_ctx, seq_lat):
    n = x.shape[0]
    tm = OUT_TILE
    ctx_tiles = n_ctx // tm
    mrow = functools.partial(_mod_row, tile=tm, n_ctx=n_ctx, seq_lat=seq_lat)

    def mspec(chunk):
        return pl.BlockSpec((None, 1, D_MODEL), lambda i: (mrow(i), 0, chunk))

    def ctx_spec(width):
        return pl.BlockSpec((tm, width), lambda i: (jnp.minimum(i, ctx_tiles - 1), 0))

    def lat_spec(width):
        return pl.BlockSpec((tm, width), lambda i: (jnp.maximum(i - ctx_tiles, 0), 0))

    return pl.pallas_call(
        functools.partial(_outproj_kernel, ctx_tiles=ctx_tiles),
        grid=(n // tm,),
        in_specs=[pl.BlockSpec((tm, D_MODEL), lambda i: (i, 0)),
                  ctx_spec(RET_W), lat_spec(RET_W), ctx_spec(H_MLA * D_V), lat_spec(H_MLA * D_V),
                  pl.BlockSpec((D_MODEL, D_MODEL), lambda i: (0, 0)),
                  mspec(2), mspec(4), mspec(3),
                  pl.BlockSpec((1, D_MODEL), lambda i: (0, 0)),
                  pl.BlockSpec((D_MODEL, LANES), lambda i: (0, 0)),
                  pl.BlockSpec((N_EXPERTS, 1), lambda i: (0, 0))],
        out_specs=[pl.BlockSpec((tm, D_MODEL), lambda i: (i, 0)),
                   pl.BlockSpec((tm, D_MODEL // 2), lambda i: (i, 0)),
                   pl.BlockSpec((2, tm), lambda i: (0, i)),
                   pl.BlockSpec((2, tm), lambda i: (0, i)),
                   pl.BlockSpec((tm, LANES), lambda i: (i, 0)),
                   pl.BlockSpec((N_EXPERTS, LANES), lambda i: (0, 0))],
        out_shape=[jax.ShapeDtypeStruct((n, D_MODEL), F32),
                   jax.ShapeDtypeStruct((n, D_MODEL // 2), I32),
                   jax.ShapeDtypeStruct((2, n), I32),
                   jax.ShapeDtypeStruct((2, n), I32),
                   jax.ShapeDtypeStruct((n, LANES), F32),
                   jax.ShapeDtypeStruct((N_EXPERTS, LANES), F32)],
        compiler_params=_params(("arbitrary",)),
        name="outproj_router",
    )(x, mix_ret[0], mix_ret[1], mix_mla[0], mix_mla[1], wo, mod3, mod3, mod3, ng, wr, br)


def _dispatch_kernel(meta_ref, dest_ref, src_ref, out_ref, sem, *, tile, nsteps, nblocks):
    def row_to(t, slot, s):
        return pltpu.make_async_copy(src_ref.at[pl.ds(t, 1)], out_ref.at[pl.ds(slot, 1)], s)

    for t in range(tile):
        row_to(t, dest_ref[0, t], sem.at[0]).start()
        row_to(t, dest_ref[1, t], sem.at[0]).start()
    for _ in range(2):
        pltpu.make_async_copy(src_ref, out_ref.at[pl.ds(0, tile)], sem.at[0]).wait()

    @pl.when(pl.program_id(0) == nsteps - 1)
    def _():
        for e in range(N_EXPERTS):
            first = meta_ref[0, e]
            npad = meta_ref[1, e]

            def fill(r, carry):
                row_to(0, first + r, sem.at[1]).start()
                return carry

            lax.fori_loop(0, npad, fill, 0)

            def fdrain(r, carry):
                row_to(0, 0, sem.at[1]).wait()
                return carry

            lax.fori_loop(0, npad, fdrain, 0)

        def tail_copy(b):
            return pltpu.make_async_copy(src_ref, out_ref.at[pl.ds(pl.multiple_of(b * tile, tile), tile)], sem.at[1])

        def tail(b, carry):
            tail_copy(b).start()
            tail_copy(b).wait()
            return carry

        lax.fori_loop(meta_ref[2, 0], nblocks, tail, 0)


def _dispatch_call(meta, dest, hp, n_slots):
    n, wcols = hp.shape
    tile = GATHER_TILE
    nsteps = n // tile
    return pl.pallas_call(
        functools.partial(_dispatch_kernel, tile=tile, nsteps=nsteps, nblocks=n_slots // tile),
        grid_spec=pltpu.PrefetchScalarGridSpec(
            num_scalar_prefetch=1, grid=(nsteps,),
            in_specs=[pl.BlockSpec((2, tile), lambda i, m: (0, i), memory_space=pltpu.SMEM),
                      pl.BlockSpec((tile, wcols), lambda i, m: (i, 0))],
            out_specs=pl.BlockSpec(memory_space=pl.ANY),
            scratch_shapes=[pltpu.SemaphoreType.DMA((2,))]),
        out_shape=jax.ShapeDtypeStruct((n_slots, wcols), I32),
        compiler_params=_params(("arbitrary",), disable_bounds_checks=True),
        name="moe_dispatch",
    )(meta, dest, hp)


def _expert_kernel(be_ref, nv_ref, x_ref, wg_ref, wu_ref, wd_ref, o_ref):
    half = D_MODEL // 2

    @pl.when(pl.program_id(0) < nv_ref[0])
    def _():
        w = x_ref[...]
        xa = lax.bitcast_convert_type(w & jnp.int32(-65536), F32).astype(BF16)
        xb = lax.bitcast_convert_type(lax.shift_left(w, 16), F32).astype(BF16)
        hg = (jnp.dot(xa, wg_ref[:half, :], preferred_element_type=F32)
              + jnp.dot(xb, wg_ref[half:, :], preferred_element_type=F32))
        hu = (jnp.dot(xa, wu_ref[:half, :], preferred_element_type=F32)
              + jnp.dot(xb, wu_ref[half:, :], preferred_element_type=F32))
        act = (_silu(hg) * hu).astype(BF16)
        o_ref[...] = jnp.dot(act, wd_ref[...], preferred_element_type=F32)

    @pl.when(pl.program_id(0) >= nv_ref[0])
    def _():
        o_ref[...] = jnp.zeros(o_ref.shape, F32)


def _expert_call(block_e, n_valid, xs, wg, wu, wd, layer):
    n_slots = xs.shape[0]
    tm = MOE_TILE
    nb = n_slots // tm

    def blk(b, be, nv):
        return jnp.minimum(b, nv[0] - 1)

    def wmap(b, be, nv):
        return (layer, be[blk(b, be, nv)], 0, 0)

    return pl.pallas_call(
        _expert_kernel,
        grid_spec=pltpu.PrefetchScalarGridSpec(
            num_scalar_prefetch=2, grid=(nb,),
            in_specs=[pl.BlockSpec((tm, D_MODEL // 2), lambda b, be, nv: (blk(b, be, nv), 0)),
                      pl.BlockSpec((None, None, D_MODEL, D_EXPERT), wmap),
                      pl.BlockSpec((None, None, D_MODEL, D_EXPERT), wmap),
                      pl.BlockSpec((None, None, D_EXPERT, D_MODEL), wmap)],
            out_specs=pl.BlockSpec((tm, D_MODEL), lambda b, be, nv: (b, 0))),
        out_shape=jax.ShapeDtypeStruct((n_slots, D_MODEL), F32),
        compiler_params=_params(("arbitrary",)),
        name="moe_experts",
    )(block_e, n_valid, xs, wg, wu, wd)


def _combine_kernel(dcur_ref, dnext_ref, o_ref, x_ref, gcol_ref, g2_ref, fg_ref, yc_ref, yl_ref, buf, sem,
                    *, tile, nsteps, ctx_tiles):
    i = pl.program_id(0)
    slot = i % 2

    def request(dref, s, k, t):
        pltpu.make_async_copy(o_ref.at[pl.ds(dref[k, t], 1)], buf.at[s, k, pl.ds(t, 1)], sem.at[s]).start()

    @pl.when(i == 0)
    def _():
        def body(t, carry):
            for k in range(2):
                request(dcur_ref, 0, k, t)
            return carry
        lax.fori_loop(0, tile, body, 0, unroll=DMA_UNROLL)

    @pl.when(i + 1 < nsteps)
    def _():
        for t in range(tile):
            for k in range(2):
                request(dnext_ref, 1 - slot, k, t)

    for k in range(2):
        pltpu.make_async_copy(o_ref.at[pl.ds(0, tile)], buf.at[slot, k], sem.at[slot]).wait()

    gc = gcol_ref[...]
    ffn = gc[:, 0:1] * buf[slot, 0] + gc[:, 1:2] * buf[slot, 1]
    x2 = x_ref[...] + g2_ref[...] * ffn
    y = x2 * lax.rsqrt(jnp.mean(x2 * x2, axis=-1, keepdims=True) + EPS) * fg_ref[...]

    @pl.when(i < ctx_tiles)
    def _():
        yc_ref[...] = y

    @pl.when(i >= ctx_tiles)
    def _():
        yl_ref[...] = y


def _combine_call(dest, o, x1, gcol, mod3, fg, n_ctx, seq_lat):
    n = x1.shape[0]
    tile = GATHER_TILE
    nsteps = n // tile
    ctx_tiles = n_ctx // tile
    mrow = functools.partial(_mod_row, tile=tile, n_ctx=n_ctx, seq_lat=seq_lat)
    out_specs = [pl.BlockSpec((tile, D_MODEL), lambda i: (jnp.minimum(i, ctx_tiles - 1), 0)),
                 pl.BlockSpec((tile, D_MODEL), lambda i: (jnp.maximum(i - ctx_tiles, 0), 0))]
    out_shape = [jax.ShapeDtypeStruct((n_ctx, D_MODEL), F32), jax.ShapeDtypeStruct((n - n_ctx, D_MODEL), F32)]
    return pl.pallas_call(
        functools.partial(_combine_kernel, tile=tile, nsteps=nsteps, ctx_tiles=ctx_tiles),
        grid=(nsteps,),
        in_specs=[pl.BlockSpec((2, tile), lambda i: (0, i), memory_space=pltpu.SMEM),
                  pl.BlockSpec((2, tile), lambda i: (0, jnp.minimum(i + 1, nsteps - 1)), memory_space=pltpu.SMEM),
                  pl.BlockSpec(memory_space=pl.ANY),
                  pl.BlockSpec((tile, D_MODEL), lambda i: (i, 0)),
                  pl.BlockSpec((tile, LANES), lambda i: (i, 0)),
                  pl.BlockSpec((None, 1, D_MODEL), lambda i: (mrow(i), 0, 5)),
                  pl.BlockSpec((1, D_MODEL), lambda i: (0, 0))],
        out_specs=out_specs,
        out_shape=out_shape,
        scratch_shapes=[pltpu.VMEM((2, 2, tile, D_MODEL), F32), pltpu.SemaphoreType.DMA((2,))],
        compiler_params=_params(("arbitrary",), disable_bounds_checks=True),
        name="moe_combine",
    )(dest, dest, o, x1, gcol, mod3, fg)


def _axial_tables(n, dim):
    t = jnp.arange(n)
    rows = (t // GRID_W).astype(F32)
    cols = (t % GRID_W).astype(F32)
    nf = dim // 4
    inv = jnp.power(ROPE_BASE, -jnp.arange(nf, dtype=F32) / nf)
    ang = jnp.concatenate([rows[:, None] * inv, cols[:, None] * inv], axis=-1)
    return jnp.cos(ang), jnp.sin(ang)


def _rope_lane_tables(n, dim):
    cos, sin = _axial_tables(n, dim)
    c = jnp.concatenate([cos, cos], axis=-1)
    s = jnp.concatenate([-sin, sin], axis=-1)
    pad = LANES - dim
    if pad:
        c = jnp.pad(c, ((0, 0), (0, pad)))
        s = jnp.pad(s, ((0, 0), (0, pad)))
    return c, s


def _identity_rows(n, dim):
    c = jnp.pad(jnp.ones((n, dim), F32), ((0, 0), (0, LANES - dim)))
    return c, jnp.zeros((n, LANES), F32)


def _swap_halves_cols(w):
    half = w.shape[-1] // 2
    return jnp.concatenate([w[..., half:], w[..., :half]], axis=-1)


def _moe_plan(eid, rank, counts, n_slots):
    tm = MOE_TILE
    counts = counts.astype(I32)
    padded = (counts + tm - 1) // tm * tm
    pend = jnp.cumsum(padded)
    pstart = pend - padded
    start_of = jnp.zeros(eid.shape, I32)
    for e in range(N_EXPERTS):
        start_of = jnp.where(eid == e, pstart[e], start_of)
    dest = (start_of + rank).astype(I32)
    n_valid = (pend[-1] // tm).astype(I32)
    blk_start = jnp.arange(n_slots // tm, dtype=I32) * tm
    block_e = jnp.minimum(jnp.sum((blk_start[:, None] >= pend[None, :]).astype(I32), axis=1), N_EXPERTS - 1)
    meta = jnp.stack([pstart + counts, padded - counts, jnp.broadcast_to(n_valid, counts.shape)]).astype(I32)
    return dest, block_e.astype(I32), n_valid.reshape(1), meta


def kernel(x_prompt, x_sample, c, cache_ckv, cache_krope, state_ret, c_ctx, w_ada, b_ada, norm_attn, norm_ffn,
           w_in, ret_decay_logit, q_norm, kv_norm, w_uq, w_ukv, w_out, w_router, b_router, w_exp_gate, w_exp_up,
           w_exp_down, final_norm):
    batch, seq, d = x_prompt.shape
    dbatch, dseq, _ = x_sample.shape
    depth = w_ada.shape[0]
    past = cache_ckv.shape[2]
    n_ctx, n_lat = batch * seq, dbatch * dseq
    n = n_ctx + n_lat
    assert d == D_MODEL and dbatch + 1 <= 8
    assert n_ctx % dseq == 0 and dseq % ROW_TILE == 0 and n_ctx % ROW_TILE == 0 and seq % RET_CHUNK == 0
    n_slots = (2 * n // MOE_TILE + N_EXPERTS) * MOE_TILE

    cond = jnp.concatenate([c_ctx[None, :], c, jnp.zeros((8 - 1 - dbatch, d), F32)], axis=0)
    mod = _ada_call(cond, w_ada, b_ada)

    cr, sr = _rope_lane_tables(dseq, D_RET)
    cm, sm = _rope_lane_tables(dseq, D_ROPE)
    one_q, zero_q = _identity_rows(ROW_TILE, D_ROPE)
    cos_q, sin_q = jnp.concatenate([one_q, cm], axis=0), jnp.concatenate([zero_q, sm], axis=0)
    one_p, zero_p = _identity_rows(past, D_ROPE)
    cos_k, sin_k = jnp.concatenate([one_p, cm], axis=0), jnp.concatenate([zero_p, sm], axis=0)
    one_c, zero_c = _identity_rows(seq, D_ROPE)
    eye = jnp.eye(LANES, dtype=BF16)
    cache_krope_p = jnp.pad(cache_krope, ((0, 0), (0, 0), (0, 0), (0, LANES - D_ROPE)))

    wr = jnp.pad(w_router, ((0, 0), (0, LANES - N_EXPERTS))).astype(BF16)
    br = b_router.reshape(N_EXPERTS, 1).astype(F32)
    lg_all = jax.nn.log_sigmoid(ret_decay_logit.astype(F32))
    wg, wu, wd = w_exp_gate.astype(BF16), w_exp_up.astype(BF16), w_exp_down.astype(BF16)

    ckv_layers, krope_layers, state_layers = [], [], []
    pending = None
    for l in range(depth):
        mod3 = mod[l].reshape(8, 1, 6 * d)
        wl = w_in[l]
        cuts = 4 * RET_W + Q_RANK + KV_RANK
        w_in_p = jnp.concatenate(
            [wl, _swap_halves_cols(wl[:, cuts:]), jnp.zeros((d, IN_W - wl.shape[1] - D_ROPE), F32)],
            axis=1).astype(BF16)
        wq = w_uq[l].reshape(Q_RANK, H_MLA, D_NOPE + D_ROPE)
        w_uq_p = jnp.concatenate([wq, _swap_halves_cols(wq[..., D_NOPE:])], axis=-1)
        w_uq_p = w_uq_p.reshape(Q_RANK, H_MLA * Q_HEAD_W).astype(BF16)
        wkv = w_ukv[l].reshape(KV_RANK, H_MLA, D_NOPE + D_V)
        wkt = wkv[..., :D_NOPE].transpose(1, 2, 0).astype(BF16)
        wv = wkv[..., D_NOPE:].transpose(1, 0, 2).astype(BF16)
        wo = w_out[l].astype(BF16)

        if pending is None:
            x, qkvg, cqn, ckv, kr = _inproj_call(x_prompt.reshape(n_ctx, d), x_sample.reshape(n_lat, d), mod3,
                                                 norm_attn[l][None, :], w_in_p, q_norm[l][None, :],
                                                 kv_norm[l][None, :], dseq)
        else:
            x, qkvg, cqn, ckv, kr = _combine_inproj_call(*pending, mod3, norm_attn[l][None, :], w_in_p,
                                                         q_norm[l][None, :], kv_norm[l][None, :], n_ctx, dseq)
        ckv_layers.append(ckv[:n_ctx].reshape(batch, seq, KV_RANK))
        krope_layers.append(kr[:n_ctx, :D_ROPE].reshape(batch, seq, D_ROPE))

        ret_c, s_ctx = _retention_call(lg_all[l], qkvg, batch, seq, 0, H_RET, seq // RET_CHUNK, emit_state=True)
        (ret_l,) = _retention_call(lg_all[l], qkvg, dbatch, dseq, n_ctx // dseq, 2, 8, s0=state_ret[:, l],
                                   tables=(cr, sr))
        state_layers.append(s_ctx)

        q = _qproj_call(cqn, w_uq_p, cos_q, sin_q, n_ctx, dseq)
        kt_c, v_c = _kvproj_call(ckv, kr, batch, seq, 0, one_c, zero_c, wkt, wv, eye, H_MLA)
        kt_l, v_l = _kvproj_call(ckv, kr, dbatch, dseq, n_ctx // dseq, cos_k, sin_k, wkt, wv, eye, 2,
                                 cache=(cache_ckv, cache_krope_p, l))
        mla_c = _attn_call(q, kt_c, v_c, seq, 0, seq, H_MLA)
        mla_l = _attn_call(q, kt_l, v_l, dseq, n_ctx, Q_TILE, 4)

        x1, hp, eid, rank, gcol, cnt = _outproj_call(x, (ret_c, ret_l), (mla_c, mla_l), wo, mod3,
                                                     norm_ffn[l][None, :], wr, br, n_ctx, dseq)

        dest, block_e, n_valid, meta = _moe_plan(eid, rank, cnt[:, 0], n_slots)
        xs = _dispatch_call(meta, dest, hp, n_slots)
        o = _expert_call(block_e, n_valid, xs, wg, wu, wd, l)
        pending = (dest, o, x1, gcol, mod3)

    y_ctx, y_lat = _combine_call(*pending, final_norm[None, :], n_ctx, dseq)
    return (y_ctx.reshape(batch, seq, d), y_lat.reshape(dbatch, dseq, d), jnp.stack(ckv_layers, axis=1),
            jnp.stack(krope_layers, axis=1), jnp.stack(state_layers, axis=1))
```

```python
import functools
import math

import jax
import jax.numpy as jnp
from jax import lax
from jax.experimental import pallas as pl
from jax.experimental.pallas import tpu as pltpu

F32 = jnp.float32
BF16 = jnp.bfloat16
I32 = jnp.int32

D_MODEL = 2048
GRID_W = 64
H_RET = 8
D_RET = 128
RET_W = H_RET * D_RET
RET_CHUNK = 128
H_MLA = 8
D_NOPE = 128
D_ROPE = 64
D_V = 128
Q_RANK = 512
KV_RANK = 256
N_EXPERTS = 16
N_GROUPS = 4
EXPERTS_PER_GROUP = N_EXPERTS // N_GROUPS
D_EXPERT = 1024
ROPE_BASE = 10000.0
EPS = 1e-6

LANES = 128
Q_HEAD_W = 2 * LANES
LAT_W = 1024
IN_W = 4 * RET_W + LAT_W
ROW_TILE = 512
OUT_TILE = 256
Q_TILE = 256
MOE_TILE = 256
GATHER_TILE = 256
DMA_UNROLL = 8
VMEM_LIMIT = 56 * 1024 * 1024


def _params(sem, **kw):
    return pltpu.CompilerParams(dimension_semantics=sem, vmem_limit_bytes=VMEM_LIMIT, **kw)


def _silu(x):
    return x * jax.nn.sigmoid(x)


def _mod_row(i, tile, n_ctx, seq_lat):
    r0 = i * tile
    return jnp.where(r0 < n_ctx, 0, 1 + (r0 - n_ctx) // seq_lat)


def _pos_block(i, tile, n_ctx, seq_lat):
    r0 = i * tile
    return jnp.where(r0 < n_ctx, 0, 1 + ((r0 - n_ctx) % seq_lat) // tile)


def _ada_kernel(c_ref, w_ref, b_ref, o_ref):
    a = _silu(c_ref[...]).astype(BF16)
    o_ref[...] = jnp.dot(a, w_ref[...].astype(BF16), preferred_element_type=F32) + b_ref[...]


def _ada_call(cond, w_ada, b_ada):
    depth, d, n6 = w_ada.shape
    tn = 1536
    return pl.pallas_call(
        _ada_kernel,
        grid=(depth, n6 // tn),
        in_specs=[pl.BlockSpec((8, d), lambda l, j: (0, 0)),
                  pl.BlockSpec((None, d, tn), lambda l, j: (l, 0, j)),
                  pl.BlockSpec((None, 1, tn), lambda l, j: (l, 0, j))],
        out_specs=pl.BlockSpec((None, 8, tn), lambda l, j: (l, 0, j)),
        out_shape=jax.ShapeDtypeStruct((depth, 8, n6), F32),
        compiler_params=_params(("arbitrary", "arbitrary")),
        name="adaln",
    )(cond, w_ada, b_ada.reshape(depth, 1, n6))


def _inproj_kernel(xc_ref, xl_ref, sh_ref, sc_ref, ng_ref, w_ref, qg_ref, kvg_ref, x_ref, qkvg_ref, cq_ref,
                   ckv_ref, kr_ref, *, ctx_tiles):
    x = jnp.where(pl.program_id(0) < ctx_tiles, xc_ref[...], xl_ref[...])
    x_ref[...] = x
    _project_rows(x, sh_ref, sc_ref, ng_ref, w_ref, qg_ref, kvg_ref, qkvg_ref, cq_ref, ckv_ref, kr_ref,
                  slice(None))


def _inproj_call(x_ctx, x_lat, mod3, ng, w_in_p, qg, kvg, seq_lat):
    n_ctx = x_ctx.shape[0]
    n = n_ctx + x_lat.shape[0]
    tm = GATHER_TILE
    ctx_tiles = n_ctx // tm
    mrow = functools.partial(_mod_row, tile=tm, n_ctx=n_ctx, seq_lat=seq_lat)
    return pl.pallas_call(
        functools.partial(_inproj_kernel, ctx_tiles=ctx_tiles),
        grid=(n // tm,),
        in_specs=[pl.BlockSpec((tm, D_MODEL), lambda i: (jnp.minimum(i, ctx_tiles - 1), 0)),
                  pl.BlockSpec((tm, D_MODEL), lambda i: (jnp.maximum(i - ctx_tiles, 0), 0)),
                  pl.BlockSpec((None, 1, D_MODEL), lambda i: (mrow(i), 0, 0)),
                  pl.BlockSpec((None, 1, D_MODEL), lambda i: (mrow(i), 0, 1)),
                  pl.BlockSpec((1, D_MODEL), lambda i: (0, 0)),
                  pl.BlockSpec((D_MODEL, IN_W), lambda i: (0, 0), pipeline_mode=pl.Buffered(1)),
                  pl.BlockSpec((1, Q_RANK), lambda i: (0, 0)),
                  pl.BlockSpec((1, KV_RANK), lambda i: (0, 0))],
        out_specs=[pl.BlockSpec((tm, D_MODEL), lambda i: (i, 0)),
                   pl.BlockSpec((tm, 4 * RET_W), lambda i: (i, 0)),
                   pl.BlockSpec((tm, Q_RANK), lambda i: (i, 0)),
                   pl.BlockSpec((tm, KV_RANK), lambda i: (i, 0)),
                   pl.BlockSpec((tm, LANES), lambda i: (i, 0))],
        out_shape=[jax.ShapeDtypeStruct((n, D_MODEL), F32),
                   jax.ShapeDtypeStruct((n, 4 * RET_W), BF16),
                   jax.ShapeDtypeStruct((n, Q_RANK), BF16),
                   jax.ShapeDtypeStruct((n, KV_RANK), F32),
                   jax.ShapeDtypeStruct((n, LANES), F32)],
        compiler_params=_params(("arbitrary",)),
        name="inproj",
    )(x_ctx, x_lat, mod3, mod3, ng, w_in_p, qg, kvg)


def _project_rows(x, sh_ref, sc_ref, ng_ref, w_ref, qg_ref, kvg_ref, qkvg_ref, cq_ref, ckv_ref, kr_ref, rs,
                  after_chunk=lambda j: None):
    y = x * lax.rsqrt(jnp.mean(x * x, axis=-1, keepdims=True) + EPS) * ng_ref[...]
    h = (y * (1.0 + sc_ref[...]) + sh_ref[...]).astype(BF16)
    for j in range(4 * RET_W // LAT_W):
        cs = slice(j * LAT_W, (j + 1) * LAT_W)
        qkvg_ref[rs, cs] = jnp.dot(h, w_ref[:, cs], preferred_element_type=F32).astype(BF16)
        after_chunk(j)
    acc = jnp.dot(h, w_ref[:, 4 * RET_W:], preferred_element_type=F32)
    cq = acc[:, :Q_RANK]
    cq = cq * lax.rsqrt(jnp.mean(cq * cq, axis=-1, keepdims=True) + EPS) * qg_ref[...]
    cq_ref[rs, :] = cq.astype(BF16)
    ckv = acc[:, Q_RANK:Q_RANK + KV_RANK]
    ckv_ref[rs, :] = ckv * lax.rsqrt(jnp.mean(ckv * ckv, axis=-1, keepdims=True) + EPS) * kvg_ref[...]
    kr_ref[rs, :] = acc[:, Q_RANK + KV_RANK:Q_RANK + KV_RANK + LANES]


def _combine_inproj_kernel(dcur_ref, dnext_ref, o_ref, x1_ref, gcol_ref, g2_ref, sh_ref, sc_ref, ng_ref, w_ref,
                           qg_ref, kvg_ref, x2_ref, qkvg_ref, cq_ref, ckv_ref, kr_ref, buf, sem, *, tile, nsteps):
    i = pl.program_id(0)
    slot = i % 2

    def row_copy(dref, s, k, t):
        return pltpu.make_async_copy(o_ref.at[pl.ds(dref[k, t], 1)], buf.at[s, k, pl.ds(t, 1)], sem.at[s])

    @pl.when(i == 0)
    def _():
        def body(t, carry):
            for k in range(2):
                row_copy(dcur_ref, 0, k, t).start()
            return carry
        lax.fori_loop(0, tile, body, 0, unroll=DMA_UNROLL)

    for k in range(2):
        pltpu.make_async_copy(o_ref.at[pl.ds(0, tile)], buf.at[slot, k], sem.at[slot]).wait()

    gc = gcol_ref[...]
    x2 = x1_ref[...] + g2_ref[...] * (gc[:, 0:1] * buf[slot, 0] + gc[:, 1:2] * buf[slot, 1])
    x2_ref[...] = x2

    nchunks = 4 * RET_W // LAT_W
    share = tile // nchunks

    def request_rows(j):
        for t in range(j * share, (j + 1) * share):
            for k in range(2):
                row_copy(dnext_ref, 1 - slot, k, t).start()

    _project_rows(x2, sh_ref, sc_ref, ng_ref, w_ref, qg_ref, kvg_ref, qkvg_ref, cq_ref, ckv_ref, kr_ref,
                  slice(None), after_chunk=request_rows)

    @pl.when(i == nsteps - 1)
    def _():
        for k in range(2):
            pltpu.make_async_copy(o_ref.at[pl.ds(0, tile)], buf.at[1 - slot, k], sem.at[1 - slot]).wait()


def _combine_inproj_call(dest, o, x1, gcol, mod_prev, mod3, ng, w_in_p, qg, kvg, n_ctx, seq_lat):
    n = x1.shape[0]
    tile = GATHER_TILE
    nsteps = n // tile
    mrow = functools.partial(_mod_row, tile=tile, n_ctx=n_ctx, seq_lat=seq_lat)

    def rows(width):
        return pl.BlockSpec((tile, width), lambda i: (i, 0))

    def const(shape):
        return pl.BlockSpec(shape, lambda i: (0, 0))

    return pl.pallas_call(
        functools.partial(_combine_inproj_kernel, tile=tile, nsteps=nsteps),
        grid=(nsteps,),
        in_specs=[pl.BlockSpec((2, tile), lambda i: (0, i), memory_space=pltpu.SMEM),
                  pl.BlockSpec((2, tile), lambda i: (0, jnp.minimum(i + 1, nsteps - 1)), memory_space=pltpu.SMEM),
                  pl.BlockSpec(memory_space=pl.ANY),
                  rows(D_MODEL), rows(LANES),
                  pl.BlockSpec((None, 1, D_MODEL), lambda i: (mrow(i), 0, 5)),
                  pl.BlockSpec((None, 1, D_MODEL), lambda i: (mrow(i), 0, 0)),
                  pl.BlockSpec((None, 1, D_MODEL), lambda i: (mrow(i), 0, 1)),
                  const((1, D_MODEL)),
                  pl.BlockSpec((D_MODEL, IN_W), lambda i: (0, 0), pipeline_mode=pl.Buffered(1)),
                  const((1, Q_RANK)), const((1, KV_RANK))],
        out_specs=[rows(D_MODEL), rows(4 * RET_W), rows(Q_RANK), rows(KV_RANK), rows(LANES)],
        out_shape=[jax.ShapeDtypeStruct((n, D_MODEL), F32),
                   jax.ShapeDtypeStruct((n, 4 * RET_W), BF16),
                   jax.ShapeDtypeStruct((n, Q_RANK), BF16),
                   jax.ShapeDtypeStruct((n, KV_RANK), F32),
                   jax.ShapeDtypeStruct((n, LANES), F32)],
        scratch_shapes=[pltpu.VMEM((2, 2, tile, D_MODEL), F32), pltpu.SemaphoreType.DMA((2,))],
        compiler_params=_params(("arbitrary",), disable_bounds_checks=True),
        name="combine_inproj",
    )(dest, dest, o, x1, gcol, mod_prev, mod3, mod3, ng, w_in_p, qg, kvg)


def _retention_kernel(lg_ref, *refs, n, hb, unroll, rope, has_s0, emit_state):
    it = iter(refs)
    q_ref, k_ref, v_ref, g_ref = next(it), next(it), next(it), next(it)
    s0_ref = next(it) if has_s0 else None
    cos_ref, sin_ref = (next(it), next(it)) if rope else (None, None)
    o_ref = next(it)
    sfin_ref = next(it) if emit_state else None
    kvf_scr, kvb_scr, kt_scr, dm_scr = next(it), next(it), next(it), next(it)

    C = RET_CHUNK
    nc = n // C
    hblk = pl.program_id(1)
    diff = (lax.broadcasted_iota(I32, (C, C), 0) - lax.broadcasted_iota(I32, (C, C), 1)).astype(F32)
    col = lax.broadcasted_iota(I32, (C, 1), 0).astype(F32)
    lane = lax.broadcasted_iota(I32, (1, C), 1).astype(F32)
    k_scale = D_RET ** -0.5

    xi_f, xi_b, zeta_f, zeta_b, g_f, g_b = [], [], [], [], [], []
    for hh in range(hb):
        lgf = lg_ref[0, hblk * hb + hh]
        lgb = lg_ref[1, hblk * hb + hh]
        dm_scr[hh] = (jnp.where(diff >= 0, jnp.exp(jnp.maximum(diff, 0.0) * lgf), 0.0)
                      + jnp.where(diff <= 0, jnp.exp(jnp.maximum(-diff, 0.0) * lgb), 0.0))
        xi_f.append(jnp.exp((col + 1.0) * lgf))
        xi_b.append(jnp.exp((C - col) * lgb))
        zeta_f.append(jnp.exp((C - 1.0 - lane) * lgf))
        zeta_b.append(jnp.exp(lane * lgb))
        g_f.append(jnp.exp(jnp.full((1, C), float(C), F32) * lgf))
        g_b.append(jnp.exp(jnp.full((1, C), float(C), F32) * lgb))

    def rows(c):
        return pl.ds(pl.multiple_of(c * C, C), C)

    def cols(hh):
        return slice(hh * C, (hh + 1) * C)

    def rot(x, c):
        if not rope:
            return x
        return x * cos_ref[rows(c), :] + pltpu.roll(x, C // 2, 1) * sin_ref[rows(c), :]

    def phase_a(c, carry):
        for hh in range(hb):
            kt = rot(k_ref[rows(c), cols(hh)].astype(F32) * k_scale, c).T
            kt_scr[hh, c] = kt.astype(BF16)
            vc = v_ref[rows(c), cols(hh)]
            kvf_scr[hh, c] = jnp.dot((kt * zeta_f[hh]).astype(BF16), vc, preferred_element_type=F32)
            kvb_scr[hh, c] = jnp.dot((kt * zeta_b[hh]).astype(BF16), vc, preferred_element_type=F32)
        return carry

    lax.fori_loop(0, nc, phase_a, 0, unroll=unroll)

    for hh in range(hb):
        def scan(i, carry, hh=hh):
            sf, sb = carry
            cb = nc - 1 - i
            upd_f = kvf_scr[hh, i]
            kvf_scr[hh, i] = sf
            upd_b = kvb_scr[hh, cb]
            kvb_scr[hh, cb] = sb
            return g_f[hh] * sf + upd_f, g_b[hh] * sb + upd_b

        if has_s0:
            init = (s0_ref[0, hh], s0_ref[1, hh])
        else:
            init = (jnp.zeros((C, C), F32), jnp.zeros((C, C), F32))
        s_f, s_b = lax.fori_loop(0, nc, scan, init)
        if emit_state:
            sfin_ref[0, hh] = s_f
            sfin_ref[1, hh] = s_b

    def phase_c(c, carry):
        for hh in range(hb):
            qc = rot(q_ref[rows(c), cols(hh)].astype(F32), c)
            inner = jnp.dot(qc.astype(BF16), kt_scr[hh, c], preferred_element_type=F32) * dm_scr[hh]
            o = (jnp.dot(inner.astype(BF16), v_ref[rows(c), cols(hh)], preferred_element_type=F32)
                 + jnp.dot((qc * xi_f[hh]).astype(BF16), kvf_scr[hh, c].astype(BF16), preferred_element_type=F32)
                 + jnp.dot((qc * xi_b[hh]).astype(BF16), kvb_scr[hh, c].astype(BF16), preferred_element_type=F32))
            dev = o - jnp.mean(o, axis=-1, keepdims=True)
            on = dev * lax.rsqrt(jnp.mean(dev * dev, axis=-1, keepdims=True) + EPS)
            gate = g_ref[rows(c), cols(hh)].astype(F32)
            o_ref[rows(c), cols(hh)] = (on * _silu(gate)).astype(BF16)
        return carry

    lax.fori_loop(0, nc, phase_c, 0, unroll=unroll)


def _retention_call(lg2, qkvg, batch, n, row_blk0, hb, unroll, *, s0=None, tables=None, emit_state=False):
    hblks = H_RET // hb
    w = hb * D_RET
    nc = n // RET_CHUNK

    def qspec(part):
        return pl.BlockSpec((n, w), lambda b, h, lg: (row_blk0 + b, part * hblks + h))

    in_specs = [qspec(0), qspec(1), qspec(2), qspec(3)]
    args = [qkvg, qkvg, qkvg, qkvg]
    if s0 is not None:
        in_specs.append(pl.BlockSpec((None, 2, hb, D_RET, D_RET), lambda b, h, lg: (b, 0, h, 0, 0)))
        args.append(s0)
    if tables is not None:
        in_specs += [pl.BlockSpec((n, D_RET), lambda b, h, lg: (0, 0))] * 2
        args += list(tables)
    out_specs = [pl.BlockSpec((n, w), lambda b, h, lg: (b, h))]
    out_shape = [jax.ShapeDtypeStruct((batch * n, RET_W), BF16)]
    if emit_state:
        out_specs.append(pl.BlockSpec((None, 2, hb, D_RET, D_RET), lambda b, h, lg: (b, 0, h, 0, 0)))
        out_shape.append(jax.ShapeDtypeStruct((batch, 2, H_RET, D_RET, D_RET), F32))
    kern = functools.partial(_retention_kernel, n=n, hb=hb, unroll=unroll, rope=tables is not None,
                             has_s0=s0 is not None, emit_state=emit_state)
    return pl.pallas_call(
        kern,
        grid_spec=pltpu.PrefetchScalarGridSpec(
            num_scalar_prefetch=1, grid=(batch, hblks), in_specs=in_specs, out_specs=out_specs,
            scratch_shapes=[pltpu.VMEM((hb, nc, D_RET, D_RET), F32), pltpu.VMEM((hb, nc, D_RET, D_RET), F32),
                            pltpu.VMEM((hb, nc, D_RET, D_RET), BF16), pltpu.VMEM((hb, D_RET, D_RET), F32)]),
        out_shape=out_shape,
        compiler_params=_params(("arbitrary", "arbitrary")),
        name="retention",
    )(lg2, *args)


def _qproj_kernel(cq_ref, w_ref, cos_ref, sin_ref, q_ref):
    scale = (D_NOPE + D_ROPE) ** -0.5 * math.log2(math.e)
    acc = jnp.dot(cq_ref[...], w_ref[...], preferred_element_type=F32)
    for h in range(H_MLA):
        base = h * Q_HEAD_W
        q_ref[:, base:base + LANES] = (acc[:, base:base + LANES] * scale).astype(BF16)
        tail = acc[:, base + LANES:base + Q_HEAD_W]
        rot = tail * cos_ref[...] + pltpu.roll(tail, D_ROPE, 1) * sin_ref[...]
        q_ref[:, base + LANES:base + Q_HEAD_W] = (rot * scale).astype(BF16)


def _qproj_call(cqn, w_uq_p, cos_q, sin_q, n_ctx, seq_lat):
    n = cqn.shape[0]
    tm = ROW_TILE
    pblk = functools.partial(_pos_block, tile=tm, n_ctx=n_ctx, seq_lat=seq_lat)
    return pl.pallas_call(
        _qproj_kernel,
        grid=(n // tm,),
        in_specs=[pl.BlockSpec((tm, Q_RANK), lambda i: (i, 0)),
                  pl.BlockSpec((Q_RANK, H_MLA * Q_HEAD_W), lambda i: (0, 0)),
                  pl.BlockSpec((tm, LANES), lambda i: (pblk(i), 0)),
                  pl.BlockSpec((tm, LANES), lambda i: (pblk(i), 0))],
        out_specs=pl.BlockSpec((tm, H_MLA * Q_HEAD_W), lambda i: (i, 0)),
        out_shape=jax.ShapeDtypeStruct((n, H_MLA * Q_HEAD_W), BF16),
        compiler_params=_params(("arbitrary",)),
        name="qproj",
    )(cqn, w_uq_p, cos_q, sin_q)


def _kvproj_kernel(*refs, hb, past):
    if past:
        cckv_ref, ckr_ref, ckv_ref, kr_ref, cos_ref, sin_ref, wkt_ref, wv_ref, eye_ref, kt_ref, v_ref = refs
        parts = [(0, past, cckv_ref, ckr_ref), (past, kt_ref.shape[-1], ckv_ref, kr_ref)]
    else:
        ckv_ref, kr_ref, cos_ref, sin_ref, wkt_ref, wv_ref, eye_ref, kt_ref, v_ref = refs
        parts = [(0, kt_ref.shape[-1], ckv_ref, kr_ref)]
    nt = (((1,), (1,)), ((), ()))
    for lo, hi, c_ref, r_ref in parts:
        keys = c_ref[...].astype(BF16)
        kr = r_ref[...]
        rot = (kr * cos_ref[lo:hi, :] + pltpu.roll(kr, D_ROPE, 1) * sin_ref[lo:hi, :]).astype(BF16)
        rope_t = lax.dot_general(eye_ref[...], rot, nt, preferred_element_type=F32).astype(BF16)
        for hh in range(hb):
            kt_ref[hh, :D_NOPE, lo:hi] = lax.dot_general(wkt_ref[hh], keys, nt,
                                                         preferred_element_type=F32).astype(BF16)
            kt_ref[hh, D_NOPE:, lo:hi] = rope_t
            v_ref[hh, lo:hi, :] = jnp.dot(keys, wv_ref[hh], preferred_element_type=F32).astype(BF16)


def _kvproj_call(ckv, kr, batch, n, row_blk0, cos_k, sin_k, wkt, wv, eye, hb, cache=None):
    hblks = H_MLA // hb
    past = 0 if cache is None else cache[0].shape[2]
    nk = past + n
    in_specs, args = [], []
    if cache is not None:
        layer = cache[2]
        in_specs += [pl.BlockSpec((None, None, past, KV_RANK), lambda b, h: (b, layer, 0, 0)),
                     pl.BlockSpec((None, None, past, LANES), lambda b, h: (b, layer, 0, 0))]
        args += [cache[0], cache[1]]
    in_specs += [pl.BlockSpec((n, KV_RANK), lambda b, h: (row_blk0 + b, 0)),
                 pl.BlockSpec((n, LANES), lambda b, h: (row_blk0 + b, 0)),
                 pl.BlockSpec((nk, LANES), lambda b, h: (0, 0)),
                 pl.BlockSpec((nk, LANES), lambda b, h: (0, 0)),
                 pl.BlockSpec((hb, D_NOPE, KV_RANK), lambda b, h: (h, 0, 0)),
                 pl.BlockSpec((hb, KV_RANK, D_V), lambda b, h: (h, 0, 0)),
                 pl.BlockSpec((LANES, LANES), lambda b, h: (0, 0))]
    args += [ckv, kr, cos_k, sin_k, wkt, wv, eye]
    return pl.pallas_call(
        functools.partial(_kvproj_kernel, hb=hb, past=past),
        grid=(batch, hblks),
        in_specs=in_specs,
        out_specs=[pl.BlockSpec((None, hb, Q_HEAD_W, nk), lambda b, h: (b, h, 0, 0)),
                   pl.BlockSpec((None, hb, nk, D_V), lambda b, h: (b, h, 0, 0))],
        out_shape=[jax.ShapeDtypeStruct((batch, H_MLA, Q_HEAD_W, nk), BF16),
                   jax.ShapeDtypeStruct((batch, H_MLA, nk, D_V), BF16)],
        compiler_params=_params(("arbitrary", "arbitrary")),
        name="kvproj",
    )(*args)


def _attn_kernel(q_ref, kt_ref, v_ref, o_ref, *, hb):
    for hh in range(hb):
        q = q_ref[:, hh * Q_HEAD_W:(hh + 1) * Q_HEAD_W]
        s = jnp.dot(q, kt_ref[hh], preferred_element_type=F32)
        p = jnp.exp2(s - jnp.max(s, axis=-1, keepdims=True))
        den = jnp.sum(p, axis=-1, keepdims=True)
        o = jnp.dot(p.astype(BF16), v_ref[hh], preferred_element_type=F32)
        o_ref[:, hh * D_V:(hh + 1) * D_V] = (o / den).astype(BF16)


def _attn_call(q, kt, v, n, row0, tq, hb):
    batch, _, _, nk = kt.shape
    hblks = H_MLA // hb
    nq = n // tq
    blk0 = row0 // tq
    return pl.pallas_call(
        functools.partial(_attn_kernel, hb=hb),
        grid=(batch, hblks, nq),
        in_specs=[pl.BlockSpec((tq, hb * Q_HEAD_W), lambda b, h, i: (blk0 + b * nq + i, h)),
                  pl.BlockSpec((None, hb, Q_HEAD_W, nk), lambda b, h, i: (b, h, 0, 0)),
                  pl.BlockSpec((None, hb, nk, D_V), lambda b, h, i: (b, h, 0, 0))],
        out_specs=pl.BlockSpec((tq, hb * D_V), lambda b, h, i: (b * nq + i, h)),
        out_shape=jax.ShapeDtypeStruct((batch * n, H_MLA * D_V), BF16),
        compiler_params=_params(("arbitrary", "arbitrary", "arbitrary")),
        name="mla_attention",
    )(q, kt, v)


def _route_rows(s_rows, sel_rows):
    epg = EXPERTS_PER_GROUP
    best_g = best_v = None
    for g in range(N_GROUPS):
        v = sel_rows[g * epg:(g + 1) * epg]
        top2 = None
        for a in range(epg):
            for b in range(a + 1, epg):
                pair = v[a] + v[b]
                top2 = pair if top2 is None else jnp.maximum(top2, pair)
        if g == 0:
            best_v, best_g = top2, jnp.zeros(top2.shape, I32)
        else:
            upd = top2 > best_v
            best_v = jnp.where(upd, top2, best_v)
            best_g = jnp.where(upd, g, best_g)

    def pick(rows_, i):
        out = rows_[i]
        for g in range(1, N_GROUPS):
            out = jnp.where(best_g == g, rows_[g * epg + i], out)
        return out

    w = [pick(sel_rows, i) for i in range(epg)]
    sv = [pick(s_rows, i) for i in range(epg)]
    m0, i0, s0 = w[0], jnp.zeros(w[0].shape, I32), sv[0]
    for i in range(1, epg):
        upd = w[i] > m0
        m0, i0, s0 = jnp.where(upd, w[i], m0), jnp.where(upd, i, i0), jnp.where(upd, sv[i], s0)
    m1 = jnp.full(w[0].shape, -jnp.inf, F32)
    i1, s1 = jnp.zeros(w[0].shape, I32), sv[0]
    for i in range(epg):
        upd = (i0 != i) & (w[i] > m1)
        m1, i1, s1 = jnp.where(upd, w[i], m1), jnp.where(upd, i, i1), jnp.where(upd, sv[i], s1)
    tot = s0 + s1
    return best_g * epg + i0, best_g * epg + i1, s0 / tot, s1 / tot


def _outproj_kernel(x_ref, mrc_ref, mrl_ref, mmc_ref, mml_ref, wo_ref, g1_ref, sc_ref, sh_ref, ng_ref, wr_ref,
                    br_ref, x1_ref, hp_ref, eid_ref, rank_ref, gcol_ref, cnt_ref, *, ctx_tiles):
    half = D_MODEL // 2

    @pl.when(pl.program_id(0) == 0)
    def _():
        cnt_ref[...] = jnp.zeros(cnt_ref.shape, F32)

    is_ctx = pl.program_id(0) < ctx_tiles
    mr = jnp.where(is_ctx, mrc_ref[...], mrl_ref[...])
    mm = jnp.where(is_ctx, mmc_ref[...], mml_ref[...])
    mix = (jnp.dot(mr, wo_ref[:RET_W, :], preferred_element_type=F32)
           + jnp.dot(mm, wo_ref[RET_W:, :], preferred_element_type=F32))
    x1 = x_ref[...] + g1_ref[...] * mix
    x1_ref[...] = x1
    y = x1 * lax.rsqrt(jnp.mean(x1 * x1, axis=-1, keepdims=True) + EPS) * ng_ref[...]
    h2 = (y * (1.0 + sc_ref[...]) + sh_ref[...]).astype(BF16)
    hi = lax.bitcast_convert_type(h2[:, :half].astype(F32), I32)
    lo = lax.bitcast_convert_type(h2[:, half:].astype(F32), I32)
    hp_ref[...] = (hi & jnp.int32(-65536)) | lax.shift_right_logical(lo, 16)

    logits = jnp.dot(h2, wr_ref[...], preferred_element_type=F32)
    lt = logits.T
    s_rows = [jax.nn.sigmoid(lt[e:e + 1, :]) for e in range(N_EXPERTS)]
    sel_rows = [s_rows[e] + br_ref[e:e + 1, :] for e in range(N_EXPERTS)]
    e0, e1, g0, g1 = _route_rows(s_rows, sel_rows)
    eid_ref[0:1, :] = e0
    eid_ref[1:2, :] = e1
    t = g0.shape[1]
    erow = lax.broadcasted_iota(I32, (N_EXPERTS, t), 0)
    hit0 = erow == e0
    hit1 = erow == e1
    onehot = jnp.where(hit0 | hit1, 1.0, 0.0)
    before = lax.broadcasted_iota(I32, (t, t), 0) < lax.broadcasted_iota(I32, (t, t), 1)
    prefix = jnp.dot(onehot.astype(BF16), jnp.where(before, 1.0, 0.0).astype(BF16), preferred_element_type=F32)
    seen = prefix + cnt_ref[:, 0:1]
    rank_ref[0:1, :] = jnp.sum(jnp.where(hit0, seen, 0.0), axis=0, keepdims=True).astype(I32)
    rank_ref[1:2, :] = jnp.sum(jnp.where(hit1, seen, 0.0), axis=0, keepdims=True).astype(I32)
    cnt_ref[...] = cnt_ref[...] + jnp.sum(onehot, axis=1, keepdims=True)
    row = lax.broadcasted_iota(I32, (LANES, t), 0)
    gcol_ref[...] = jnp.where(row == 0, g0, jnp.where(row == 1, g1, 0.0)).T


def _outproj_call(x, mix_ret, mix_mla, wo, mod3, ng, wr, br, n_ctx, seq_lat):
    n = x.shape[0]
    tm = OUT_TILE
    ctx_tiles = n_ctx // tm
    mrow = functools.partial(_mod_row, tile=tm, n_ctx=n_ctx, seq_lat=seq_lat)

    def mspec(chunk):
        return pl.BlockSpec((None, 1, D_MODEL), lambda i: (mrow(i), 0, chunk))

    def ctx_spec(width):
        return pl.BlockSpec((tm, width), lambda i: (jnp.minimum(i, ctx_tiles - 1), 0))

    def lat_spec(width):
        return pl.BlockSpec((tm, width), lambda i: (jnp.maximum(i - ctx_tiles, 0), 0))

    return pl.pallas_call(
        functools.partial(_outproj_kernel, ctx_tiles=ctx_tiles),
        grid=(n // tm,),
        in_specs=[pl.BlockSpec((tm, D_MODEL), lambda i: (i, 0)),
                  ctx_spec(RET_W), lat_spec(RET_W), ctx_spec(H_MLA * D_V), lat_spec(H_MLA * D_V),
                  pl.BlockSpec((D_MODEL, D_MODEL), lambda i: (0, 0)),
                  mspec(2), mspec(4), mspec(3),
                  pl.BlockSpec((1, D_MODEL), lambda i: (0, 0)),
                  pl.BlockSpec((D_MODEL, LANES), lambda i: (0, 0)),
                  pl.BlockSpec((N_EXPERTS, 1), lambda i: (0, 0))],
        out_specs=[pl.BlockSpec((tm, D_MODEL), lambda i: (i, 0)),
                   pl.BlockSpec((tm, D_MODEL // 2), lambda i: (i, 0)),
                   pl.BlockSpec((2, tm), lambda i: (0, i)),
                   pl.BlockSpec((2, tm), lambda i: (0, i)),
                   pl.BlockSpec((tm, LANES), lambda i: (i, 0)),
                   pl.BlockSpec((N_EXPERTS, LANES), lambda i: (0, 0))],
        out_shape=[jax.ShapeDtypeStruct((n, D_MODEL), F32),
                   jax.ShapeDtypeStruct((n, D_MODEL // 2), I32),
                   jax.ShapeDtypeStruct((2, n), I32),
                   jax.ShapeDtypeStruct((2, n), I32),
                   jax.ShapeDtypeStruct((n, LANES), F32),
                   jax.ShapeDtypeStruct((N_EXPERTS, LANES), F32)],
        compiler_params=_params(("arbitrary",)),
        name="outproj_router",
    )(x, mix_ret[0], mix_ret[1], mix_mla[0], mix_mla[1], wo, mod3, mod3, mod3, ng, wr, br)


def _dispatch_kernel(meta_ref, dest_ref, src_ref, out_ref, sem, *, tile, nsteps, nblocks):
    def row_to(t, slot, s):
        return pltpu.make_async_copy(src_ref.at[pl.ds(t, 1)], out_ref.at[pl.ds(slot, 1)], s)

    for t in range(tile):
        for k in range(2):
            row_to(t, dest_ref[k, t], sem.at[0]).start(priority=k)
    for _ in range(2):
        pltpu.make_async_copy(src_ref, out_ref.at[pl.ds(0, tile)], sem.at[0]).wait()

    @pl.when(pl.program_id(0) == nsteps - 1)
    def _():
        for e in range(N_EXPERTS):
            first = meta_ref[0, e]
            npad = meta_ref[1, e]

            def fill(r, carry):
                row_to(0, first + r, sem.at[1]).start()
                return carry

            lax.fori_loop(0, npad, fill, 0)

            def fdrain(r, carry):
                row_to(0, 0, sem.at[1]).wait()
                return carry

            lax.fori_loop(0, npad, fdrain, 0)

        def tail_copy(b):
            return pltpu.make_async_copy(src_ref, out_ref.at[pl.ds(pl.multiple_of(b * tile, tile), tile)], sem.at[1])

        def tail(b, carry):
            tail_copy(b).start()
            tail_copy(b).wait()
            return carry

        lax.fori_loop(meta_ref[2, 0], nblocks, tail, 0)


def _dispatch_call(meta, dest, hp, n_slots):
    n, wcols = hp.shape
    tile = GATHER_TILE
    nsteps = n // tile
    return pl.pallas_call(
        functools.partial(_dispatch_kernel, tile=tile, nsteps=nsteps, nblocks=n_slots // tile),
        grid_spec=pltpu.PrefetchScalarGridSpec(
            num_scalar_prefetch=1, grid=(nsteps,),
            in_specs=[pl.BlockSpec((2, tile), lambda i, m: (0, i), memory_space=pltpu.SMEM),
                      pl.BlockSpec((tile, wcols), lambda i, m: (i, 0))],
            out_specs=pl.BlockSpec(memory_space=pl.ANY),
            scratch_shapes=[pltpu.SemaphoreType.DMA((2,))]),
        out_shape=jax.ShapeDtypeStruct((n_slots, wcols), I32),
        compiler_params=_params(("arbitrary",), disable_bounds_checks=True),
        name="moe_dispatch",
    )(meta, dest, hp)


def _expert_kernel(be_ref, nv_ref, x_ref, wg_ref, wu_ref, wd_ref, o_ref):
    half = D_MODEL // 2

    @pl.when(pl.program_id(0) < nv_ref[0])
    def _():
        w = x_ref[...]
        xa = lax.bitcast_convert_type(w & jnp.int32(-65536), F32).astype(BF16)
        xb = lax.bitcast_convert_type(lax.shift_left(w, 16), F32).astype(BF16)
        hg = (jnp.dot(xa, wg_ref[:half, :], preferred_element_type=F32)
              + jnp.dot(xb, wg_ref[half:, :], preferred_element_type=F32))
        hu = (jnp.dot(xa, wu_ref[:half, :], preferred_element_type=F32)
              + jnp.dot(xb, wu_ref[half:, :], preferred_element_type=F32))
        act = (_silu(hg) * hu).astype(BF16)
        o_ref[...] = jnp.dot(act, wd_ref[...], preferred_element_type=F32)

    @pl.when(pl.program_id(0) >= nv_ref[0])
    def _():
        o_ref[...] = jnp.zeros(o_ref.shape, F32)


def _expert_call(block_e, n_valid, xs, wg, wu, wd, layer):
    n_slots = xs.shape[0]
    tm = MOE_TILE
    nb = n_slots // tm

    def blk(b, be, nv):
        return jnp.minimum(b, nv[0] - 1)

    def wmap(b, be, nv):
        return (layer, be[blk(b, be, nv)], 0, 0)

    return pl.pallas_call(
        _expert_kernel,
        grid_spec=pltpu.PrefetchScalarGridSpec(
            num_scalar_prefetch=2, grid=(nb,),
            in_specs=[pl.BlockSpec((tm, D_MODEL // 2), lambda b, be, nv: (blk(b, be, nv), 0)),
                      pl.BlockSpec((None, None, D_MODEL, D_EXPERT), wmap),
                      pl.BlockSpec((None, None, D_MODEL, D_EXPERT), wmap),
                      pl.BlockSpec((None, None, D_EXPERT, D_MODEL), wmap)],
            out_specs=pl.BlockSpec((tm, D_MODEL), lambda b, be, nv: (b, 0))),
        out_shape=jax.ShapeDtypeStruct((n_slots, D_MODEL), F32),
        compiler_params=_params(("arbitrary",)),
        name="moe_experts",
    )(block_e, n_valid, xs, wg, wu, wd)


def _combine_kernel(dcur_ref, dnext_ref, o_ref, x_ref, gcol_ref, g2_ref, fg_ref, yc_ref, yl_ref, buf, sem,
                    *, tile, nsteps, ctx_tiles):
    i = pl.program_id(0)
    slot = i % 2

    def request(dref, s, k, t):
        pltpu.make_async_copy(o_ref.at[pl.ds(dref[k, t], 1)], buf.at[s, k, pl.ds(t, 1)], sem.at[s]).start()

    @pl.when(i == 0)
    def _():
        def body(t, carry):
            for k in range(2):
                request(dcur_ref, 0, k, t)
            return carry
        lax.fori_loop(0, tile, body, 0, unroll=DMA_UNROLL)

    @pl.when(i + 1 < nsteps)
    def _():
        for t in range(tile):
            for k in range(2):
                request(dnext_ref, 1 - slot, k, t)

    for k in range(2):
        pltpu.make_async_copy(o_ref.at[pl.ds(0, tile)], buf.at[slot, k], sem.at[slot]).wait()

    gc = gcol_ref[...]
    ffn = gc[:, 0:1] * buf[slot, 0] + gc[:, 1:2] * buf[slot, 1]
    x2 = x_ref[...] + g2_ref[...] * ffn
    y = x2 * lax.rsqrt(jnp.mean(x2 * x2, axis=-1, keepdims=True) + EPS) * fg_ref[...]

    @pl.when(i < ctx_tiles)
    def _():
        yc_ref[...] = y

    @pl.when(i >= ctx_tiles)
    def _():
        yl_ref[...] = y


def _combine_call(dest, o, x1, gcol, mod3, fg, n_ctx, seq_lat):
    n = x1.shape[0]
    tile = GATHER_TILE
    nsteps = n // tile
    ctx_tiles = n_ctx // tile
    mrow = functools.partial(_mod_row, tile=tile, n_ctx=n_ctx, seq_lat=seq_lat)
    out_specs = [pl.BlockSpec((tile, D_MODEL), lambda i: (jnp.minimum(i, ctx_tiles - 1), 0)),
                 pl.BlockSpec((tile, D_MODEL), lambda i: (jnp.maximum(i - ctx_tiles, 0), 0))]
    out_shape = [jax.ShapeDtypeStruct((n_ctx, D_MODEL), F32), jax.ShapeDtypeStruct((n - n_ctx, D_MODEL), F32)]
    return pl.pallas_call(
        functools.partial(_combine_kernel, tile=tile, nsteps=nsteps, ctx_tiles=ctx_tiles),
        grid=(nsteps,),
        in_specs=[pl.BlockSpec((2, tile), lambda i: (0, i), memory_space=pltpu.SMEM),
                  pl.BlockSpec((2, tile), lambda i: (0, jnp.minimum(i + 1, nsteps - 1)), memory_space=pltpu.SMEM),
                  pl.BlockSpec(memory_space=pl.ANY),
                  pl.BlockSpec((tile, D_MODEL), lambda i: (i, 0)),
                  pl.BlockSpec((tile, LANES), lambda i: (i, 0)),
                  pl.BlockSpec((None, 1, D_MODEL), lambda i: (mrow(i), 0, 5)),
                  pl.BlockSpec((1, D_MODEL), lambda i: (0, 0))],
        out_specs=out_specs,
        out_shape=out_shape,
        scratch_shapes=[pltpu.VMEM((2, 2, tile, D_MODEL), F32), pltpu.SemaphoreType.DMA((2,))],
        compiler_params=_params(("arbitrary",), disable_bounds_checks=True),
        name="moe_combine",
    )(dest, dest, o, x1, gcol, mod3, fg)


def _axial_tables(n, dim):
    t = jnp.arange(n)
    rows = (t // GRID_W).astype(F32)
    cols = (t % GRID_W).astype(F32)
    nf = dim // 4
    inv = jnp.power(ROPE_BASE, -jnp.arange(nf, dtype=F32) / nf)
    ang = jnp.concatenate([rows[:, None] * inv, cols[:, None] * inv], axis=-1)
    return jnp.cos(ang), jnp.sin(ang)


def _rope_lane_tables(n, dim):
    cos, sin = _axial_tables(n, dim)
    c = jnp.concatenate([cos, cos], axis=-1)
    s = jnp.concatenate([-sin, sin], axis=-1)
    pad = LANES - dim
    if pad:
        c = jnp.pad(c, ((0, 0), (0, pad)))
        s = jnp.pad(s, ((0, 0), (0, pad)))
    return c, s


def _identity_rows(n, dim):
    c = jnp.pad(jnp.ones((n, dim), F32), ((0, 0), (0, LANES - dim)))
    return c, jnp.zeros((n, LANES), F32)


def _swap_halves_cols(w):
    half = w.shape[-1] // 2
    return jnp.concatenate([w[..., half:], w[..., :half]], axis=-1)


def _moe_plan(eid, rank, counts, n_slots):
    tm = MOE_TILE
    counts = counts.astype(I32)
    padded = (counts + tm - 1) // tm * tm
    pend = jnp.cumsum(padded)
    pstart = pend - padded
    start_of = jnp.zeros(eid.shape, I32)
    for e in range(N_EXPERTS):
        start_of = jnp.where(eid == e, pstart[e], start_of)
    dest = (start_of + rank).astype(I32)
    n_valid = (pend[-1] // tm).astype(I32)
    blk_start = jnp.arange(n_slots // tm, dtype=I32) * tm
    block_e = jnp.minimum(jnp.sum((blk_start[:, None] >= pend[None, :]).astype(I32), axis=1), N_EXPERTS - 1)
    meta = jnp.stack([pstart + counts, padded - counts, jnp.broadcast_to(n_valid, counts.shape)]).astype(I32)
    return dest, block_e.astype(I32), n_valid.reshape(1), meta


def kernel(x_prompt, x_sample, c, cache_ckv, cache_krope, state_ret, c_ctx, w_ada, b_ada, norm_attn, norm_ffn,
           w_in, ret_decay_logit, q_norm, kv_norm, w_uq, w_ukv, w_out, w_router, b_router, w_exp_gate, w_exp_up,
           w_exp_down, final_norm):
    batch, seq, d = x_prompt.shape
    dbatch, dseq, _ = x_sample.shape
    depth = w_ada.shape[0]
    past = cache_ckv.shape[2]
    n_ctx, n_lat = batch * seq, dbatch * dseq
    n = n_ctx + n_lat
    assert d == D_MODEL and dbatch + 1 <= 8
    assert n_ctx % dseq == 0 and dseq % ROW_TILE == 0 and n_ctx % ROW_TILE == 0 and seq % RET_CHUNK == 0
    n_slots = (2 * n // MOE_TILE + N_EXPERTS) * MOE_TILE

    cond = jnp.concatenate([c_ctx[None, :], c, jnp.zeros((8 - 1 - dbatch, d), F32)], axis=0)
    mod = _ada_call(cond, w_ada, b_ada)

    cr, sr = _rope_lane_tables(dseq, D_RET)
    cm, sm = _rope_lane_tables(dseq, D_ROPE)
    one_q, zero_q = _identity_rows(ROW_TILE, D_ROPE)
    cos_q, sin_q = jnp.concatenate([one_q, cm], axis=0), jnp.concatenate([zero_q, sm], axis=0)
    one_p, zero_p = _identity_rows(past, D_ROPE)
    cos_k, sin_k = jnp.concatenate([one_p, cm], axis=0), jnp.concatenate([zero_p, sm], axis=0)
    one_c, zero_c = _identity_rows(seq, D_ROPE)
    eye = jnp.eye(LANES, dtype=BF16)
    cache_krope_p = jnp.pad(cache_krope, ((0, 0), (0, 0), (0, 0), (0, LANES - D_ROPE)))

    wr = jnp.pad(w_router, ((0, 0), (0, LANES - N_EXPERTS))).astype(BF16)
    br = b_router.reshape(N_EXPERTS, 1).astype(F32)
    lg_all = jax.nn.log_sigmoid(ret_decay_logit.astype(F32))
    wg, wu, wd = w_exp_gate.astype(BF16), w_exp_up.astype(BF16), w_exp_down.astype(BF16)

    ckv_layers, krope_layers, state_layers = [], [], []
    pending = None
    for l in range(depth):
        mod3 = mod[l].reshape(8, 1, 6 * d)
        wl = w_in[l]
        cuts = 4 * RET_W + Q_RANK + KV_RANK
        w_in_p = jnp.concatenate(
            [wl, _swap_halves_cols(wl[:, cuts:]), jnp.zeros((d, IN_W - wl.shape[1] - D_ROPE), F32)],
            axis=1).astype(BF16)
        wq = w_uq[l].reshape(Q_RANK, H_MLA, D_NOPE + D_ROPE)
        w_uq_p = jnp.concatenate([wq, _swap_halves_cols(wq[..., D_NOPE:])], axis=-1)
        w_uq_p = w_uq_p.reshape(Q_RANK, H_MLA * Q_HEAD_W).astype(BF16)
        wkv = w_ukv[l].reshape(KV_RANK, H_MLA, D_NOPE + D_V)
        wkt = wkv[..., :D_NOPE].transpose(1, 2, 0).astype(BF16)
        wv = wkv[..., D_NOPE:].transpose(1, 0, 2).astype(BF16)
        wo = w_out[l].astype(BF16)

        if pending is None:
            x, qkvg, cqn, ckv, kr = _inproj_call(x_prompt.reshape(n_ctx, d), x_sample.reshape(n_lat, d), mod3,
                                                 norm_attn[l][None, :], w_in_p, q_norm[l][None, :],
                                                 kv_norm[l][None, :], dseq)
        else:
            x, qkvg, cqn, ckv, kr = _combine_inproj_call(*pending, mod3, norm_attn[l][None, :], w_in_p,
                                                         q_norm[l][None, :], kv_norm[l][None, :], n_ctx, dseq)
        ckv_layers.append(ckv[:n_ctx].reshape(batch, seq, KV_RANK))
        krope_layers.append(kr[:n_ctx, :D_ROPE].reshape(batch, seq, D_ROPE))

        ret_c, s_ctx = _retention_call(lg_all[l], qkvg, batch, seq, 0, H_RET, seq // RET_CHUNK, emit_state=True)
        (ret_l,) = _retention_call(lg_all[l], qkvg, dbatch, dseq, n_ctx // dseq, 2, 8, s0=state_ret[:, l],
                                   tables=(cr, sr))
        state_layers.append(s_ctx)

        q = _qproj_call(cqn, w_uq_p, cos_q, sin_q, n_ctx, dseq)
        kt_c, v_c = _kvproj_call(ckv, kr, batch, seq, 0, one_c, zero_c, wkt, wv, eye, H_MLA)
        kt_l, v_l = _kvproj_call(ckv, kr, dbatch, dseq, n_ctx // dseq, cos_k, sin_k, wkt, wv, eye, 2,
                                 cache=(cache_ckv, cache_krope_p, l))
        mla_c = _attn_call(q, kt_c, v_c, seq, 0, seq, H_MLA)
        mla_l = _attn_call(q, kt_l, v_l, dseq, n_ctx, Q_TILE, 4)

        x1, hp, eid, rank, gcol, cnt = _outproj_call(x, (ret_c, ret_l), (mla_c, mla_l), wo, mod3,
                                                     norm_ffn[l][None, :], wr, br, n_ctx, dseq)

        dest, block_e, n_valid, meta = _moe_plan(eid, rank, cnt[:, 0], n_slots)
        xs = _dispatch_call(meta, dest, hp, n_slots)
        o = _expert_call(block_e, n_valid, xs, wg, wu, wd, l)
        pending = (dest, o, x1, gcol, mod3)

    y_ctx, y_lat = _combine_call(*pending, final_norm[None, :], n_ctx, dseq)
    return (y_ctx.reshape(batch, seq, d), y_lat.reshape(dbatch, dseq, d), jnp.stack(ckv_layers, axis=1),
            jnp.stack(krope_layers, axis=1), jnp.stack(state_layers, axis=1))
```
